```python
import jax, jax.numpy as jnp
from jax import lax
import numpy as np

D_MODEL = 2048
BATCH = 4
SEQ = 2048
DEPTH = 4
DEC_BATCH = 2
DEC_SEQ = 4096
PAST_LEN = 128

N_MIXERS = 2
N_A_LAYERS = (DEPTH + 1) // 2
N_B_LAYERS = DEPTH // 2
D_FF = 5632
NORM_EPS = 1e-6
Q_BLOCK = 128
NEG_INF = -1e30
DIL_GROUPS = ((128, 1), (512, 4), (2048, 16))
N_GROUPS_A = 3
HEADS_PER_GROUP_A = 8
HEAD_DIM_A = 128
QKV_WIDTH_A = 3 * N_GROUPS_A * HEADS_PER_GROUP_A * HEAD_DIM_A
OUT_WIDTH_A = HEADS_PER_GROUP_A * HEAD_DIM_A
N_HEADS_B = 16
Q_LORA_RANK = 768
KV_LORA_RANK = 512
QK_NOPE_DIM = 128
QK_ROPE_DIM = 64
V_HEAD_DIM = 128
ROPE_THETA = 10000.0
IN_WIDTH_B = Q_LORA_RANK + KV_LORA_RANK + QK_ROPE_DIM

kernel_name = 'hybrid_dilated_mla_macaron_encoder'


def rmsnorm(x, g):
    x32 = x.astype(jnp.float32)
    y = x32 * lax.rsqrt(jnp.mean(x32 * x32, axis=-1, keepdims=True) + NORM_EPS)
    return (y * g.astype(jnp.float32)).astype(x.dtype)


def swiglu_ffn(h, w_gate, w_up, w_down):
    return (jax.nn.silu(h @ w_gate) * (h @ w_up)) @ w_down


def alibi_slopes():
    n = N_GROUPS_A * HEADS_PER_GROUP_A
    h = jnp.arange(1, n + 1, dtype=jnp.float32)
    return jnp.exp2(-8.0 * h / n).reshape(N_GROUPS_A, HEADS_PER_GROUP_A)


def dilated_window_attention(q, k, v, slopes, window, dilation):
    b, h, s, hd = q.shape
    radius = window // (2 * dilation)
    offsets = np.arange(-radius, radius + 1, dtype=np.int32) * dilation
    pad = radius * dilation
    k_pad = jnp.pad(k, ((0, 0), (0, 0), (pad, pad), (0, 0)))
    v_pad = jnp.pad(v, ((0, 0), (0, 0), (pad, pad), (0, 0)))
    bias = -slopes[:, None, None] * jnp.abs(jnp.asarray(offsets, jnp.float32))[None, None, :]
    n_blocks = s // Q_BLOCK
    q_blocks = q.reshape(b, h, n_blocks, Q_BLOCK, hd).transpose(2, 0, 1, 3, 4)
    starts = jnp.arange(n_blocks, dtype=jnp.int32) * Q_BLOCK
    scale = hd ** -0.5

    def block(args):
        qb, start = args
        pos = start + jnp.arange(Q_BLOCK, dtype=jnp.int32)
        key_pos = pos[:, None] + offsets[None, :]
        valid = (key_pos >= 0) & (key_pos < s)
        kb = jnp.take(k_pad, key_pos + pad, axis=2)
        vb = jnp.take(v_pad, key_pos + pad, axis=2)
        sc = jnp.einsum('bhqd,bhqkd->bhqk', qb, kb).astype(jnp.float32) * scale + bias
        sc = jnp.where(valid, sc, NEG_INF)
        m = jnp.max(sc, axis=-1, keepdims=True)
        p = jnp.exp(sc - m)
        l = jnp.sum(p, axis=-1, keepdims=True)
        o = jnp.einsum('bhqk,bhqkd->bhqd', (p / l).astype(vb.dtype), vb)
        return o, (m + jnp.log(l))[..., 0]

    o, lse = lax.map(block, (q_blocks, starts))
    o = o.transpose(1, 2, 0, 3, 4).reshape(b, h, s, hd)
    lse = lse.transpose(1, 2, 0, 3).reshape(b, h, s)
    return o, lse


def dilated_mixture_mixer(h, w_qkv, w_o):
    b, s, _ = h.shape
    qkv = (h @ w_qkv).reshape(b, s, 3, N_GROUPS_A, HEADS_PER_GROUP_A, HEAD_DIM_A)
    qkv = qkv.transpose(2, 3, 0, 4, 1, 5)
    slopes = alibi_slopes()
    outs, lses = [], []
    for g, (window, dilation) in enumerate(DIL_GROUPS):
        o, lse = dilated_window_attention(qkv[0, g], qkv[1, g], qkv[2, g], slopes[g], window, dilation)
        outs.append(o)
        lses.append(lse)
    alpha = jax.nn.softmax(jnp.stack(lses), axis=0)
    o = jnp.einsum('gbhs,gbhsd->bshd', alpha, jnp.stack(outs).astype(jnp.float32))
    return o.reshape(b, s, OUT_WIDTH_A).astype(h.dtype) @ w_o


def apply_rope(x, cos, sin):
    x32 = x.astype(jnp.float32)
    x1, x2 = jnp.split(x32, 2, axis=-1)
    c = cos[:, None, :]
    sn = sin[:, None, :]
    return jnp.concatenate([x1 * c - x2 * sn, x1 * sn + x2 * c], axis=-1).astype(x.dtype)


def latent_attention_mixer(h, w_in, q_norm, w_uq, kv_norm, w_ukv, w_o):
    b, s, _ = h.shape
    c = h @ w_in
    c_q, c_kv, k_rope = jnp.split(c, [Q_LORA_RANK, Q_LORA_RANK + KV_LORA_RANK], axis=-1)
    q = (rmsnorm(c_q, q_norm) @ w_uq).reshape(b, s, N_HEADS_B, QK_NOPE_DIM + QK_ROPE_DIM)
    kv = (rmsnorm(c_kv, kv_norm) @ w_ukv).reshape(b, s, N_HEADS_B, QK_NOPE_DIM + V_HEAD_DIM)
    q_nope, q_rope = jnp.split(q, [QK_NOPE_DIM], axis=-1)
    k_nope, v = jnp.split(kv, [QK_NOPE_DIM], axis=-1)
    pos = jnp.arange(s, dtype=jnp.float32)
    inv_freq = ROPE_THETA ** (-jnp.arange(0, QK_ROPE_DIM, 2, dtype=jnp.float32) / QK_ROPE_DIM)
    ang = pos[:, None] * inv_freq[None, :]
    cos, sin = jnp.cos(ang), jnp.sin(ang)
    q_rope = apply_rope(q_rope, cos, sin)
    k_rope = apply_rope(k_rope[:, :, None, :], cos, sin)[:, :, 0, :]
    scale = (QK_NOPE_DIM + QK_ROPE_DIM) ** -0.5
    n_blocks = s // Q_BLOCK

    def to_blocks(t):
        return t.reshape(b, n_blocks, Q_BLOCK, *t.shape[2:]).swapaxes(0, 1)

    def block(args):
        qn, qr = args
        sc = (jnp.einsum('bqhd,bkhd->bhqk', qn, k_nope).astype(jnp.float32)
              + jnp.einsum('bqhr,bkr->bhqk', qr, k_rope).astype(jnp.float32)) * scale
        p = jax.nn.softmax(sc, axis=-1)
        return jnp.einsum('bhqk,bkhd->bqhd', p.astype(v.dtype), v)

    o = lax.map(block, (to_blocks(q_nope), to_blocks(q_rope)))
    o = o.swapaxes(0, 1).reshape(b, s, N_HEADS_B * V_HEAD_DIM)
    return o @ w_o


def encoder_trunk(x, p):
    for i in range(DEPTH):
        x = x + 0.5 * swiglu_ffn(rmsnorm(x, p['ffn1_norm'][i]), p['ffn1_w_gate'][i], p['ffn1_w_up'][i], p['ffn1_w_down'][i])
        h = rmsnorm(x, p['mix_norm'][i])
        j = i // N_MIXERS
        if i % N_MIXERS == 0:
            x = x + dilated_mixture_mixer(h, p['a_w_qkv'][j], p['a_w_o'][j])
        else:
            x = x + latent_attention_mixer(h, p['b_w_in'][j], p['b_q_norm'][j], p['b_w_uq'][j],
                                           p['b_kv_norm'][j], p['b_w_ukv'][j], p['b_w_o'][j])
        x = x + 0.5 * swiglu_ffn(rmsnorm(x, p['ffn2_norm'][i]), p['ffn2_w_gate'][i], p['ffn2_w_up'][i], p['ffn2_w_down'][i])
    return rmsnorm(x, p['final_norm'])


def _normal(key, shape, fan_in):
    return jax.random.normal(key, shape, jnp.float32) * (fan_in ** -0.5)


def _gain(key, shape):
    return 1.0 + 0.05 * jax.random.normal(key, shape, jnp.float32)


def setup_inputs(seed: int = 0) -> dict:
    key = jax.random.key(seed)
    ks = jax.random.split(key, 20)
    return {
        'x_prompt': jax.random.normal(ks[0], (BATCH, SEQ, D_MODEL), jnp.float32),
        'x_sample': jax.random.normal(ks[1], (DEC_BATCH, DEC_SEQ, D_MODEL), jnp.float32),
        'ffn1_norm': _gain(ks[2], (DEPTH, D_MODEL)),
        'ffn1_w_gate': _normal(ks[3], (DEPTH, D_MODEL, D_FF), D_MODEL),
        'ffn1_w_up': _normal(ks[4], (DEPTH, D_MODEL, D_FF), D_MODEL),
        'ffn1_w_down': _normal(ks[5], (DEPTH, D_FF, D_MODEL), D_FF),
        'mix_norm': _gain(ks[6], (DEPTH, D_MODEL)),
        'a_w_qkv': _normal(ks[7], (N_A_LAYERS, D_MODEL, QKV_WIDTH_A), D_MODEL),
        'a_w_o': _normal(ks[8], (N_A_LAYERS, OUT_WIDTH_A, D_MODEL), OUT_WIDTH_A),
        'b_w_in': _normal(ks[9], (N_B_LAYERS, D_MODEL, IN_WIDTH_B), D_MODEL),
        'b_q_norm': _gain(ks[10], (N_B_LAYERS, Q_LORA_RANK)),
        'b_w_uq': _normal(ks[11], (N_B_LAYERS, Q_LORA_RANK, N_HEADS_B * (QK_NOPE_DIM + QK_ROPE_DIM)), Q_LORA_RANK),
        'b_kv_norm': _gain(ks[12], (N_B_LAYERS, KV_LORA_RANK)),
        'b_w_ukv': _normal(ks[13], (N_B_LAYERS, KV_LORA_RANK, N_HEADS_B * (QK_NOPE_DIM + V_HEAD_DIM)), KV_LORA_RANK),
        'b_w_o': _normal(ks[14], (N_B_LAYERS, N_HEADS_B * V_HEAD_DIM, D_MODEL), N_HEADS_B * V_HEAD_DIM),
        'ffn2_norm': _gain(ks[15], (DEPTH, D_MODEL)),
        'ffn2_w_gate': _normal(ks[16], (DEPTH, D_MODEL, D_FF), D_MODEL),
        'ffn2_w_up': _normal(ks[17], (DEPTH, D_MODEL, D_FF), D_MODEL),
        'ffn2_w_down': _normal(ks[18], (DEPTH, D_FF, D_MODEL), D_FF),
        'final_norm': _gain(ks[19], (D_MODEL,)),
    }


def reference(x_prompt, x_sample, ffn1_norm, ffn1_w_gate, ffn1_w_up, ffn1_w_down, mix_norm,
              a_w_qkv, a_w_o, b_w_in, b_q_norm, b_w_uq, b_kv_norm, b_w_ukv, b_w_o,
              ffn2_norm, ffn2_w_gate, ffn2_w_up, ffn2_w_down, final_norm):
    params = dict(ffn1_norm=ffn1_norm, ffn1_w_gate=ffn1_w_gate, ffn1_w_up=ffn1_w_up, ffn1_w_down=ffn1_w_down,
                  mix_norm=mix_norm, a_w_qkv=a_w_qkv, a_w_o=a_w_o,
                  b_w_in=b_w_in, b_q_norm=b_q_norm, b_w_uq=b_w_uq, b_kv_norm=b_kv_norm,
                  b_w_ukv=b_w_ukv, b_w_o=b_w_o,
                  ffn2_norm=ffn2_norm, ffn2_w_gate=ffn2_w_gate, ffn2_w_up=ffn2_w_up, ffn2_w_down=ffn2_w_down,
                  final_norm=final_norm)
    y_prompt = encoder_trunk(x_prompt, params)
    y_sample = encoder_trunk(x_sample, params)
    return (y_prompt, y_sample)
```

```python
import functools

import numpy as np
import jax
import jax.numpy as jnp
from jax import lax
from jax.experimental import pallas as pl
from jax.experimental.pallas import tpu as pltpu

F32 = jnp.float32
BF16 = jnp.bfloat16

NORM_EPS = 1e-6
NEG_INF = -1e30
LANES = 128

DIL_GROUPS = ((128, 1), (512, 4), (2048, 16))
N_GROUPS_A = 3
HEADS_A = 8
HEAD_DIM_A = 128
GROUP_WIDTH_A = HEADS_A * HEAD_DIM_A
QKV_WIDTH_A = 3 * N_GROUPS_A * GROUP_WIDTH_A
Q_BLOCK_A = 128
N_HEADS_B = 16
Q_LORA_RANK = 768
KV_LORA_RANK = 512
QK_NOPE_DIM = 128
QK_ROPE_DIM = 64
V_HEAD_DIM = 128
ROPE_THETA = 10000.0
QK_PAD_B = 256

VMEM_LIMIT_BYTES = 48 * 1024 * 1024


def _params(*sem):
    return pltpu.CompilerParams(dimension_semantics=sem, vmem_limit_bytes=VMEM_LIMIT_BYTES)


def _rmsnorm(x, g):
    ms = jnp.mean(x * x, axis=-1, keepdims=True)
    return x * lax.rsqrt(ms + NORM_EPS) * g


def _ffn_kernel(*refs, final):
    if final:
        x_ref, g_ref, wg_ref, wu_ref, wd_ref, fg_ref, o_ref, h_ref, acc_ref = refs
    else:
        x_ref, g_ref, wg_ref, wu_ref, wd_ref, o_ref, h_ref, acc_ref = refs
    j = pl.program_id(1)

    @pl.when(j == 0)
    def _():
        h_ref[...] = _rmsnorm(x_ref[...], g_ref[...]).astype(BF16)

    h = h_ref[...]
    gate = jnp.dot(h, wg_ref[...], preferred_element_type=F32)
    up = jnp.dot(h, wu_ref[...], preferred_element_type=F32)
    act = (gate * jax.nn.sigmoid(gate) * up).astype(BF16)
    part = jnp.dot(act, wd_ref[...], preferred_element_type=F32)

    @pl.when(j == 0)
    def _():
        acc_ref[...] = part

    @pl.when(j > 0)
    def _():
        acc_ref[...] += part

    @pl.when(j == pl.num_programs(1) - 1)
    def _():
        y = x_ref[...] + 0.5 * acc_ref[...]
        if final:
            y = _rmsnorm(y, fg_ref[...])
        o_ref[...] = y


def _ffn(x, g, wg, wu, wd, final_g=None, *, tm=512, tf=512):
    m, d = x.shape
    f = wg.shape[1]
    final = final_g is not None
    in_specs = [
        pl.BlockSpec((tm, d), lambda i, j: (i, 0)),
        pl.BlockSpec((1, d), lambda i, j: (0, 0)),
        pl.BlockSpec((d, tf), lambda i, j: (0, j)),
        pl.BlockSpec((d, tf), lambda i, j: (0, j)),
        pl.BlockSpec((tf, d), lambda i, j: (j, 0)),
    ]
    args = [x, g.reshape(1, d), wg, wu, wd]
    if final:
        in_specs.append(pl.BlockSpec((1, d), lambda i, j: (0, 0)))
        args.append(final_g.reshape(1, d))
    return pl.pallas_call(
        functools.partial(_ffn_kernel, final=final),
        grid=(m // tm, f // tf),
        in_specs=in_specs,
        out_specs=pl.BlockSpec((tm, d), lambda i, j: (i, 0)),
        out_shape=jax.ShapeDtypeStruct((m, d), F32),
        scratch_shapes=[pltpu.VMEM((tm, d), BF16), pltpu.VMEM((tm, d), F32)],
        compiler_params=_params("parallel", "arbitrary"),
        name="ffn",
    )(*args)


def _norm_proj_kernel(x_ref, g_ref, w_ref, o_ref, h_ref):
    @pl.when(pl.program_id(1) == 0)
    def _():
        h_ref[...] = _rmsnorm(x_ref[...], g_ref[...]).astype(BF16)

    o_ref[...] = jnp.dot(h_ref[...], w_ref[...], preferred_element_type=F32).astype(o_ref.dtype)


def _norm_proj(x, g, w, out_dtype, *, tm, tn):
    m, d = x.shape
    n = w.shape[1]
    return pl.pallas_call(
        _norm_proj_kernel,
        grid=(m // tm, n // tn),
        in_specs=[
            pl.BlockSpec((tm, d), lambda i, j: (i, 0)),
            pl.BlockSpec((1, d), lambda i, j: (0, 0)),
            pl.BlockSpec((d, tn), lambda i, j: (0, j)),
        ],
        out_specs=pl.BlockSpec((tm, tn), lambda i, j: (i, j)),
        out_shape=jax.ShapeDtypeStruct((m, n), out_dtype),
        scratch_shapes=[pltpu.VMEM((tm, d), BF16)],
        compiler_params=_params("parallel", "arbitrary"),
        name="norm_proj",
    )(x, g.reshape(1, d), w)


def _window_attn_kernel(q_ref, k_ref, v_ref, bias_ref, o_ref, lse_ref, kpad_ref, vpad_ref, *, sub_len, heads_per_block):
    half = Q_BLOCK_A // 2
    hblk = pl.program_id(2)
    n_q = sub_len // Q_BLOCK_A
    scale = HEAD_DIM_A ** -0.5

    @pl.when(hblk == 0)
    def _():
        lse_ref[...] = jnp.zeros_like(lse_ref)

    zeros = jnp.zeros((half, HEAD_DIM_A), BF16)
    lane = lax.broadcasted_iota(jnp.int32, (Q_BLOCK_A, LANES), 1)
    col = lax.broadcasted_iota(jnp.int32, (Q_BLOCK_A, 2 * Q_BLOCK_A), 1)
    for h in range(heads_per_block):
        hs = slice(h * HEAD_DIM_A, (h + 1) * HEAD_DIM_A)
        kpad_ref[0:half, :] = zeros
        kpad_ref[half + sub_len:2 * half + sub_len, :] = zeros
        kpad_ref[half:half + sub_len, :] = k_ref[:, hs]
        vpad_ref[0:half, :] = zeros
        vpad_ref[half + sub_len:2 * half + sub_len, :] = zeros
        vpad_ref[half:half + sub_len, :] = v_ref[:, hs]
        bias = bias_ref[h]
        head_lane = hblk * heads_per_block + h

        def body(i, carry):
            q0 = pl.multiple_of(i * Q_BLOCK_A, Q_BLOCK_A)
            q = q_ref[pl.ds(q0, Q_BLOCK_A), hs]
            kw = kpad_ref[pl.ds(q0, 2 * Q_BLOCK_A), :]
            vw = vpad_ref[pl.ds(q0, 2 * Q_BLOCK_A), :]
            s = lax.dot_general(q, kw, (((1,), (1,)), ((), ())), preferred_element_type=F32)
            s = s * scale + bias
            key_pos = col + (q0 - half)
            s = jnp.where((key_pos >= 0) & (key_pos < sub_len), s, NEG_INF)
            m = jnp.max(s, axis=-1, keepdims=True)
            p = jnp.exp(s - m)
            l = jnp.sum(p, axis=-1, keepdims=True)
            o = jnp.dot((p / l).astype(BF16), vw, preferred_element_type=F32)
            o_ref[pl.ds(q0, Q_BLOCK_A), hs] = o
            lse = m + jnp.log(l)
            cur = lse_ref[pl.ds(q0, Q_BLOCK_A), :]
            lse_ref[pl.ds(q0, Q_BLOCK_A), :] = jnp.where(lane == head_lane, lse, cur)
            return carry

        lax.fori_loop(0, n_q, body, 0)


def _alibi_band_bias(group, dilation):
    n = N_GROUPS_A * HEADS_A
    head = jnp.arange(1, n + 1, dtype=F32)
    slopes = jnp.exp2(-8.0 * head / n).reshape(N_GROUPS_A, HEADS_A)[group]
    r = np.arange(Q_BLOCK_A)[:, None]
    c = np.arange(2 * Q_BLOCK_A)[None, :]
    rel = np.abs(c - r - Q_BLOCK_A // 2)
    dist = jnp.asarray(rel * dilation, F32)
    bias = -slopes[:, None, None] * dist[None]
    return jnp.where(jnp.asarray(rel <= Q_BLOCK_A // 2)[None], bias, NEG_INF)


def _window_attn(qkv, group, seq_len):
    m = qkv.shape[0]
    n_seq = m // seq_len
    _, d = DIL_GROUPS[group]
    sub_len = seq_len // d
    hb = max(1, min(HEADS_A, 8192 // sub_len))
    bw = hb * HEAD_DIM_A
    n_hblk = HEADS_A // hb
    qkv_v = qkv.reshape(m // d, d * QKV_WIDTH_A)
    per_r = QKV_WIDTH_A // bw
    per_g = GROUP_WIDTH_A // bw

    def in_map(t):
        return lambda b, r, hh: (b, r * per_r + (t * N_GROUPS_A + group) * per_g + hh)

    o, lse = pl.pallas_call(
        functools.partial(_window_attn_kernel, sub_len=sub_len, heads_per_block=hb),
        grid=(n_seq, d, n_hblk),
        in_specs=[
            pl.BlockSpec((sub_len, bw), in_map(0)),
            pl.BlockSpec((sub_len, bw), in_map(1)),
            pl.BlockSpec((sub_len, bw), in_map(2)),
            pl.BlockSpec((hb, Q_BLOCK_A, 2 * Q_BLOCK_A), lambda b, r, hh: (hh, 0, 0)),
        ],
        out_specs=[
            pl.BlockSpec((sub_len, bw), lambda b, r, hh: (b, r * per_g + hh)),
            pl.BlockSpec((sub_len, LANES), lambda b, r, hh: (b, r)),
        ],
        out_shape=[
            jax.ShapeDtypeStruct((m // d, d * GROUP_WIDTH_A), F32),
            jax.ShapeDtypeStruct((m // d, d * LANES), F32),
        ],
        scratch_shapes=[
            pltpu.VMEM((sub_len + Q_BLOCK_A, HEAD_DIM_A), BF16),
            pltpu.VMEM((sub_len + Q_BLOCK_A, HEAD_DIM_A), BF16),
        ],
        compiler_params=_params("parallel", "parallel", "arbitrary"),
        name=f"window_attn_g{group}",
    )(qkv_v, qkv_v, qkv_v, _alibi_band_bias(group, d))
    return o.reshape(m, GROUP_WIDTH_A), lse.reshape(m, LANES)


def _mix_out_proj_kernel(x_ref, o0_ref, o1_ref, o2_ref, l0_ref, l1_ref, l2_ref, w_ref, y_ref):
    l0, l1, l2 = l0_ref[...], l1_ref[...], l2_ref[...]
    mx = jnp.maximum(jnp.maximum(l0, l1), l2)
    e0, e1, e2 = jnp.exp(l0 - mx), jnp.exp(l1 - mx), jnp.exp(l2 - mx)
    den = e0 + e1 + e2
    a0, a1, a2 = e0 / den, e1 / den, e2 / den
    tm = x_ref.shape[0]
    cols = []
    for h in range(HEADS_A):
        hs = slice(h * HEAD_DIM_A, (h + 1) * HEAD_DIM_A)
        shape = (tm, HEAD_DIM_A)
        mixed = (jnp.broadcast_to(a0[:, h:h + 1], shape) * o0_ref[:, hs]
                 + jnp.broadcast_to(a1[:, h:h + 1], shape) * o1_ref[:, hs]
                 + jnp.broadcast_to(a2[:, h:h + 1], shape) * o2_ref[:, hs])
        cols.append(mixed.astype(BF16))
    mixed = jnp.concatenate(cols, axis=-1)
    y_ref[...] = x_ref[...] + jnp.dot(mixed, w_ref[...], preferred_element_type=F32)


def _mix_out_proj(x, outs, lses, w_o, *, tm=256):
    m, d = x.shape
    row = lambda i: (i, 0)
    return pl.pallas_call(
        _mix_out_proj_kernel,
        grid=(m // tm,),
        in_specs=[pl.BlockSpec((tm, d), row)]
        + [pl.BlockSpec((tm, GROUP_WIDTH_A), row)] * 3
        + [pl.BlockSpec((tm, LANES), row)] * 3
        + [pl.BlockSpec((GROUP_WIDTH_A, d), lambda i: (0, 0))],
        out_specs=pl.BlockSpec((tm, d), row),
        out_shape=jax.ShapeDtypeStruct((m, d), F32),
        compiler_params=_params("parallel"),
        name="mix_out_proj",
    )(x, *outs, *lses, w_o)


def _latent_q_kernel(c_ref, g_ref, w_ref, cos_ref, sin_ref, q_ref):
    h = _rmsnorm(c_ref[...], g_ref[...]).astype(BF16)
    full = jnp.dot(h, w_ref[...], preferred_element_type=F32)
    cos_t, sin_t = cos_ref[...], sin_ref[...]
    per = QK_NOPE_DIM + 2 * LANES
    for hd in range(N_HEADS_B):
        base = hd * per
        nope = full[:, base:base + QK_NOPE_DIM]
        rope = full[:, base + QK_NOPE_DIM:base + QK_NOPE_DIM + LANES]
        swapped = full[:, base + QK_NOPE_DIM + LANES:base + per]
        q_ref[:, hd * QK_PAD_B:hd * QK_PAD_B + QK_NOPE_DIM] = nope.astype(BF16)
        q_ref[:, hd * QK_PAD_B + QK_NOPE_DIM:(hd + 1) * QK_PAD_B] = (rope * cos_t + swapped * sin_t).astype(BF16)


def _latent_kv_kernel(c_ref, kr_ref, krs_ref, g_ref, w_ref, cos_ref, sin_ref, k_ref, v_ref):
    h = _rmsnorm(c_ref[...], g_ref[...]).astype(BF16)
    kv = jnp.dot(h, w_ref[...], preferred_element_type=F32)
    k_rope = (kr_ref[...] * cos_ref[...] + krs_ref[...] * sin_ref[...]).astype(BF16)
    per = QK_NOPE_DIM + V_HEAD_DIM
    for hd in range(N_HEADS_B):
        k_ref[:, hd * QK_PAD_B:hd * QK_PAD_B + QK_NOPE_DIM] = kv[:, hd * per:hd * per + QK_NOPE_DIM].astype(BF16)
        k_ref[:, hd * QK_PAD_B + QK_NOPE_DIM:(hd + 1) * QK_PAD_B] = k_rope
        v_ref[:, hd * V_HEAD_DIM:(hd + 1) * V_HEAD_DIM] = kv[:, hd * per + QK_NOPE_DIM:(hd + 1) * per].astype(BF16)


def _latent_attn_kernel(q_ref, k_ref, v_ref, o_ref):
    scale = (QK_NOPE_DIM + QK_ROPE_DIM) ** -0.5
    s = lax.dot_general(q_ref[...], k_ref[...], (((1,), (1,)), ((), ())), preferred_element_type=F32) * scale
    m = jnp.max(s, axis=-1, keepdims=True)
    p = jnp.exp(s - m)
    l = jnp.sum(p, axis=-1, keepdims=True)
    o = jnp.dot(p.astype(BF16), v_ref[...], preferred_element_type=F32)
    o_ref[...] = (o / l).astype(o_ref.dtype)


def _rope_tables(seq_len):
    pos = jnp.arange(seq_len, dtype=F32)
    inv_freq = ROPE_THETA ** (-jnp.arange(0, QK_ROPE_DIM, 2, dtype=F32) / QK_ROPE_DIM)
    ang = pos[:, None] * inv_freq[None, :]
    cos, sin = jnp.cos(ang), jnp.sin(ang)
    zeros = jnp.zeros((seq_len, LANES - QK_ROPE_DIM), F32)
    return jnp.concatenate([cos, cos, zeros], axis=-1), jnp.concatenate([-sin, sin, zeros], axis=-1)


def _latent_weights(w_in, w_uq):
    half = QK_ROPE_DIM // 2
    d = w_in.shape[0]
    c_q = w_in[:, :Q_LORA_RANK]
    c_kv = w_in[:, Q_LORA_RANK:Q_LORA_RANK + KV_LORA_RANK]
    x1 = w_in[:, Q_LORA_RANK + KV_LORA_RANK:Q_LORA_RANK + KV_LORA_RANK + half]
    x2 = w_in[:, Q_LORA_RANK + KV_LORA_RANK + half:]
    zpad = jnp.zeros((d, LANES - QK_ROPE_DIM), w_in.dtype)
    w_in_p = jnp.concatenate([c_kv, x1, x2, zpad, x2, x1, zpad, c_q], axis=-1)

    w = w_uq.reshape(Q_LORA_RANK, N_HEADS_B, QK_NOPE_DIM + QK_ROPE_DIM)
    nope = w[:, :, :QK_NOPE_DIM]
    q1 = w[:, :, QK_NOPE_DIM:QK_NOPE_DIM + half]
    q2 = w[:, :, QK_NOPE_DIM + half:]
    zq = jnp.zeros((Q_LORA_RANK, N_HEADS_B, LANES - QK_ROPE_DIM), w_uq.dtype)
    w_uq_p = jnp.concatenate([nope, q1, q2, zq, q2, q1, zq], axis=-1).reshape(Q_LORA_RANK, -1)
    return w_in_p, w_uq_p


def _latent_qkv(c, q_norm, kv_norm, w_uq_p, w_ukv, cos_t, sin_t, seq_len, *, tm=256):
    m = c.shape[0]
    n_pos_blocks = seq_len // tm
    row = lambda i: (i, 0)
    pos = lambda i: (i % n_pos_blocks, 0)
    const = lambda i: (0, 0)
    q = pl.pallas_call(
        _latent_q_kernel,
        grid=(m // tm,),
        in_specs=[
            pl.BlockSpec((tm, Q_LORA_RANK), lambda i: (i, 1)),
            pl.BlockSpec((1, Q_LORA_RANK), const),
            pl.BlockSpec(w_uq_p.shape, const),
            pl.BlockSpec((tm, LANES), pos),
            pl.BlockSpec((tm, LANES), pos),
        ],
        out_specs=pl.BlockSpec((tm, N_HEADS_B * QK_PAD_B), row),
        out_shape=jax.ShapeDtypeStruct((m, N_HEADS_B * QK_PAD_B), BF16),
        compiler_params=_params("parallel"),
        name="latent_q",
    )(c, q_norm.reshape(1, -1), w_uq_p, cos_t, sin_t)
    kr_block = KV_LORA_RANK // LANES
    k, v = pl.pallas_call(
        _latent_kv_kernel,
        grid=(m // tm,),
        in_specs=[
            pl.BlockSpec((tm, KV_LORA_RANK), row),
            pl.BlockSpec((tm, LANES), lambda i: (i, kr_block)),
            pl.BlockSpec((tm, LANES), lambda i: (i, kr_block + 1)),
            pl.BlockSpec((1, KV_LORA_RANK), const),
            pl.BlockSpec(w_ukv.shape, const),
            pl.BlockSpec((tm, LANES), pos),
            pl.BlockSpec((tm, LANES), pos),
        ],
        out_specs=[
            pl.BlockSpec((tm, N_HEADS_B * QK_PAD_B), row),
            pl.BlockSpec((tm, N_HEADS_B * V_HEAD_DIM), row),
        ],
        out_shape=[
            jax.ShapeDtypeStruct((m, N_HEADS_B * QK_PAD_B), BF16),
            jax.ShapeDtypeStruct((m, N_HEADS_B * V_HEAD_DIM), BF16),
        ],
        compiler_params=_params("parallel"),
        name="latent_kv",
    )(c, c, c, kv_norm.reshape(1, -1), w_ukv, cos_t, sin_t)
    return q, k, v


def _latent_attn(q, k, v, seq_len, *, tq=256):
    m = q.shape[0]
    n_seq = m // seq_len
    n_q = seq_len // tq
    return pl.pallas_call(
        _latent_attn_kernel,
        grid=(n_seq, N_HEADS_B, n_q),
        in_specs=[
            pl.BlockSpec((tq, QK_PAD_B), lambda b, h, i: (b * n_q + i, h)),
            pl.BlockSpec((seq_len, QK_PAD_B), lambda b, h, i: (b, h)),
            pl.BlockSpec((seq_len, V_HEAD_DIM), lambda b, h, i: (b, h)),
        ],
        out_specs=pl.BlockSpec((tq, V_HEAD_DIM), lambda b, h, i: (b * n_q + i, h)),
        out_shape=jax.ShapeDtypeStruct((m, N_HEADS_B * V_HEAD_DIM), BF16),
        compiler_params=_params("parallel", "parallel", "arbitrary"),
        name="latent_attn",
    )(q, k, v)


def _out_proj_kernel(x_ref, a_ref, w_ref, y_ref):
    y_ref[...] = x_ref[...] + jnp.dot(a_ref[...], w_ref[...], preferred_element_type=F32)


def _out_proj(x, a, w, *, tm=256):
    m, d = x.shape
    k = a.shape[1]
    row = lambda i: (i, 0)
    return pl.pallas_call(
        _out_proj_kernel,
        grid=(m // tm,),
        in_specs=[pl.BlockSpec((tm, d), row), pl.BlockSpec((tm, k), row), pl.BlockSpec((k, d), lambda i: (0, 0))],
        out_specs=pl.BlockSpec((tm, d), row),
        out_shape=jax.ShapeDtypeStruct((m, d), F32),
        compiler_params=_params("parallel"),
        name="out_proj",
    )(x, a, w)


def _dilated_mixture_mixer(x, g, w_qkv, w_o, seq_len):
    qkv = _norm_proj(x, g, w_qkv, BF16, tm=512, tn=1024)
    outs, lses = [], []
    for group in range(N_GROUPS_A):
        o, lse = _window_attn(qkv, group, seq_len)
        outs.append(o)
        lses.append(lse)
    return _mix_out_proj(x, outs, lses, w_o)


def _latent_attention_mixer(x, g, w_in_p, q_norm, w_uq_p, kv_norm, w_ukv, w_o, seq_len):
    c = _norm_proj(x, g, w_in_p, F32, tm=512, tn=w_in_p.shape[1])
    cos_t, sin_t = _rope_tables(seq_len)
    q, k, v = _latent_qkv(c, q_norm, kv_norm, w_uq_p, w_ukv, cos_t, sin_t, seq_len)
    o = _latent_attn(q, k, v, seq_len)
    return _out_proj(x, o, w_o)


def _trunk(x, seq_len, p):
    depth = p['ffn1_norm'].shape[0]
    for i in range(depth):
        x = _ffn(x, p['ffn1_norm'][i], p['ffn1_w_gate'][i], p['ffn1_w_up'][i], p['ffn1_w_down'][i])
        j = i // 2
        if i % 2 == 0:
            x = _dilated_mixture_mixer(x, p['mix_norm'][i], p['a_w_qkv'][j], p['a_w_o'][j], seq_len)
        else:
            x = _latent_attention_mixer(x, p['mix_norm'][i], p['b_w_in'][j], p['b_q_norm'][j], p['b_w_uq'][j],
                                        p['b_kv_norm'][j], p['b_w_ukv'][j], p['b_w_o'][j], seq_len)
        final_g = p['final_norm'] if i == depth - 1 else None
        x = _ffn(x, p['ffn2_norm'][i], p['ffn2_w_gate'][i], p['ffn2_w_up'][i], p['ffn2_w_down'][i], final_g)
    return x


def kernel(x_prompt, x_sample, ffn1_norm, ffn1_w_gate, ffn1_w_up, ffn1_w_down, mix_norm, a_w_qkv, a_w_o, b_w_in, b_q_norm, b_w_uq, b_kv_norm, b_w_ukv, b_w_o, ffn2_norm, ffn2_w_gate, ffn2_w_up, ffn2_w_down, final_norm):
    n_b = b_w_in.shape[0]
    latent = [_latent_weights(b_w_in[j], b_w_uq[j]) for j in range(n_b)]
    p = dict(
        ffn1_norm=ffn1_norm, ffn2_norm=ffn2_norm, mix_norm=mix_norm, final_norm=final_norm,
        b_q_norm=b_q_norm, b_kv_norm=b_kv_norm,
        ffn1_w_gate=ffn1_w_gate.astype(BF16), ffn1_w_up=ffn1_w_up.astype(BF16), ffn1_w_down=ffn1_w_down.astype(BF16),
        ffn2_w_gate=ffn2_w_gate.astype(BF16), ffn2_w_up=ffn2_w_up.astype(BF16), ffn2_w_down=ffn2_w_down.astype(BF16),
        a_w_qkv=a_w_qkv.astype(BF16), a_w_o=a_w_o.astype(BF16),
        b_w_in=[w.astype(BF16) for w, _ in latent], b_w_uq=[w.astype(BF16) for _, w in latent],
        b_w_ukv=b_w_ukv.astype(BF16), b_w_o=b_w_o.astype(BF16),
    )
    outs = []
    for x in (x_prompt, x_sample):
        b, s, d = x.shape
        outs.append(_trunk(x.reshape(b * s, d), s, p).reshape(b, s, d))
    return tuple(outs)
```

```python
import functools
import math

import numpy as np
import jax
import jax.numpy as jnp
from jax import lax
from jax.experimental import pallas as pl
from jax.experimental.pallas import tpu as pltpu

F32 = jnp.float32
BF16 = jnp.bfloat16

NORM_EPS = 1e-6
NEG_INF = -1e30
LANES = 128

DIL_GROUPS = ((128, 1), (512, 4), (2048, 16))
N_GROUPS_A = 3
HEADS_A = 8
HEAD_DIM_A = 128
GROUP_WIDTH_A = HEADS_A * HEAD_DIM_A
Q_BLOCK_A = 128
CHAINS_A = 4
N_HEADS_B = 16
Q_LORA_RANK = 768
KV_LORA_RANK = 512
QK_NOPE_DIM = 128
QK_ROPE_DIM = 64
V_HEAD_DIM = 128
ROPE_THETA = 10000.0
QK_PAD_B = 256

VMEM_LIMIT_BYTES = 48 * 1024 * 1024


def _params(*sem):
    return pltpu.CompilerParams(dimension_semantics=sem, vmem_limit_bytes=VMEM_LIMIT_BYTES)


def _rmsnorm(x, g):
    ms = jnp.mean(x * x, axis=-1, keepdims=True)
    return x * lax.rsqrt(ms + NORM_EPS) * g


def _ffn_kernel(*refs, final):
    if final:
        x_ref, g_ref, wg_ref, wu_ref, wd_ref, fg_ref, o_ref, h_ref, acc_ref = refs
    else:
        x_ref, g_ref, wg_ref, wu_ref, wd_ref, o_ref, h_ref, acc_ref = refs
    j = pl.program_id(1)

    @pl.when(j == 0)
    def _():
        h_ref[...] = _rmsnorm(x_ref[...], g_ref[...]).astype(BF16)
        acc_ref[...] = jnp.zeros_like(acc_ref)

    h = h_ref[...]
    gate = jnp.dot(h, wg_ref[...], preferred_element_type=F32)
    up = jnp.dot(h, wu_ref[...], preferred_element_type=F32)
    act = (gate * jax.nn.sigmoid(gate) * up).astype(BF16)
    acc_ref[...] += jnp.dot(act, wd_ref[...], preferred_element_type=F32)

    @pl.when(j == pl.num_programs(1) - 1)
    def _():
        y = x_ref[...] + 0.5 * acc_ref[...]
        if final:
            y = _rmsnorm(y, fg_ref[...])
        o_ref[...] = y


def _ffn(x, g, wg, wu, wd, layer, final_g=None, *, tm=512, tf=512):
    m, d = x.shape
    f = wg.shape[2]
    final = final_g is not None
    in_specs = [
        pl.BlockSpec((tm, d), lambda i, j: (i, 0)),
        pl.BlockSpec((None, 1, d), lambda i, j: (layer, 0, 0)),
        pl.BlockSpec((None, d, tf), lambda i, j: (layer, 0, j)),
        pl.BlockSpec((None, d, tf), lambda i, j: (layer, 0, j)),
        pl.BlockSpec((None, tf, d), lambda i, j: (layer, j, 0)),
    ]
    args = [x, g.reshape(-1, 1, d), wg, wu, wd]
    if final:
        in_specs.append(pl.BlockSpec((1, d), lambda i, j: (0, 0)))
        args.append(final_g.reshape(1, d))
    return pl.pallas_call(
        functools.partial(_ffn_kernel, final=final),
        grid=(m // tm, f // tf),
        in_specs=in_specs,
        out_specs=pl.BlockSpec((tm, d), lambda i, j: (i, 0)),
        out_shape=jax.ShapeDtypeStruct((m, d), F32),
        scratch_shapes=[pltpu.VMEM((tm, d), BF16), pltpu.VMEM((tm, d), F32)],
        compiler_params=_params("parallel", "arbitrary"),
        name="ffn",
    )(*args)


def _norm_proj_kernel(x_ref, g_ref, w_ref, o_ref, h_ref):
    @pl.when(pl.program_id(1) == 0)
    def _():
        h_ref[...] = _rmsnorm(x_ref[...], g_ref[...]).astype(BF16)

    o_ref[...] = jnp.dot(h_ref[...], w_ref[...], preferred_element_type=F32).astype(o_ref.dtype)


def _norm_proj(x, g, w, out_dtype, *, tm, tn):
    m, d = x.shape
    n = w.shape[1]
    return pl.pallas_call(
        _norm_proj_kernel,
        grid=(m // tm, n // tn),
        in_specs=[
            pl.BlockSpec((tm, d), lambda i, j: (i, 0)),
            pl.BlockSpec((1, d), lambda i, j: (0, 0)),
            pl.BlockSpec((d, tn), lambda i, j: (0, j)),
        ],
        out_specs=pl.BlockSpec((tm, tn), lambda i, j: (i, j)),
        out_shape=jax.ShapeDtypeStruct((m, n), out_dtype),
        scratch_shapes=[pltpu.VMEM((tm, d), BF16)],
        compiler_params=_params("parallel", "arbitrary"),
        name="norm_proj",
    )(x, g.reshape(1, d), w)


def _qkv_proj_kernel(x_ref, g_ref, perm_ref, w_ref, o0_ref, o1_ref, o2_ref, hnat_ref, hperm_ref):
    j = pl.program_id(1)

    @pl.when(j == 0)
    def _():
        h = _rmsnorm(x_ref[...], g_ref[...]).astype(BF16)
        hnat_ref[...] = h
        hperm_ref[...] = h

    @pl.when((j == 3) | (j == 6))
    def _():
        hperm_ref[...] = jnp.dot(perm_ref[...], hnat_ref[...], preferred_element_type=F32).astype(BF16)

    res = jnp.dot(hperm_ref[...], w_ref[...], preferred_element_type=F32)
    tm = res.shape[0]
    for group, o_ref in enumerate((o0_ref, o1_ref, o2_ref)):
        dil = DIL_GROUPS[group][1]
        rows = tm // dil

        @pl.when(j // 3 == group)
        def _(o_ref=o_ref, dil=dil, rows=rows):
            for hd in range(HEADS_A):
                for r in range(dil):
                    o_ref[hd, r] = res[r * rows:(r + 1) * rows, hd * HEAD_DIM_A:(hd + 1) * HEAD_DIM_A].astype(BF16)


def _residue_permutations(tm):
    mats = []
    for _, dil in DIL_GROUPS:
        rows = tm // dil
        p = np.zeros((tm, tm), np.float32)
        dst = np.arange(tm)
        src = (dst % rows) * dil + dst // rows
        p[dst, src] = 1.0
        mats.append(p)
    return jnp.asarray(np.stack(mats), BF16)


def _qkv_proj(x, g, w_qkv, g_layer, w_layer, seq_len, *, tm=512):
    m, d_model = x.shape
    n_seq = m // seq_len
    tiles_per_seq = seq_len // tm
    n_col = 3 * N_GROUPS_A

    def out_spec(group):
        dil = DIL_GROUPS[group][1]

        def index(i, j):
            t = jnp.clip(j - 3 * group, 0, 2)
            return (t, 0, i // tiles_per_seq, 0, i % tiles_per_seq, 0)

        return pl.BlockSpec((None, HEADS_A, None, dil, tm // dil, HEAD_DIM_A), index)

    out_shape = [
        jax.ShapeDtypeStruct((3, HEADS_A, n_seq, dil, seq_len // dil, HEAD_DIM_A), BF16) for _, dil in DIL_GROUPS
    ]
    return pl.pallas_call(
        _qkv_proj_kernel,
        grid=(m // tm, n_col),
        in_specs=[
            pl.BlockSpec((tm, d_model), lambda i, j: (i, 0)),
            pl.BlockSpec((None, 1, d_model), lambda i, j: (g_layer, 0, 0)),
            pl.BlockSpec((None, tm, tm), lambda i, j: (j // 3, 0, 0)),
            pl.BlockSpec((None, d_model, GROUP_WIDTH_A), lambda i, j: (w_layer, 0, (j % 3) * N_GROUPS_A + j // 3)),
        ],
        out_specs=[out_spec(group) for group in range(N_GROUPS_A)],
        out_shape=out_shape,
        scratch_shapes=[pltpu.VMEM((tm, d_model), BF16), pltpu.VMEM((tm, d_model), BF16)],
        compiler_params=_params("parallel", "arbitrary"),
        name="qkv_proj",
    )(x, g.reshape(-1, 1, d_model), _residue_permutations(tm), w_qkv)


def _window_attn_kernel(q_ref, k_ref, v_ref, bias_ref, o_ref, lse_ref, kpad_ref, vpad_ref, *,
                        sub_len, dilation, n_res):
    half = Q_BLOCK_A // 2
    head = pl.program_id(1)
    res_blk = pl.program_id(2)
    n_q = sub_len // Q_BLOCK_A
    scale = HEAD_DIM_A ** -0.5

    @pl.when((head == 0) & (res_blk == 0))
    def _():
        lse_ref[...] = jnp.zeros_like(lse_ref)

    zeros = jnp.zeros((half, HEAD_DIM_A), BF16)
    for rr in range(n_res):
        kpad_ref[rr, 0:half, :] = zeros
        kpad_ref[rr, half + sub_len:2 * half + sub_len, :] = zeros
        kpad_ref[rr, half:half + sub_len, :] = k_ref[rr]
        vpad_ref[rr, 0:half, :] = zeros
        vpad_ref[rr, half + sub_len:2 * half + sub_len, :] = zeros
        vpad_ref[rr, half:half + sub_len, :] = v_ref[rr]

    lane = lax.broadcasted_iota(jnp.int32, (Q_BLOCK_A, LANES), 1)
    col = lax.broadcasted_iota(jnp.int32, (Q_BLOCK_A, 2 * Q_BLOCK_A), 1)
    bias = bias_ref[...]

    def one_block(rr, q0):
        q = q_ref[rr, pl.ds(q0, Q_BLOCK_A), :]
        kw = kpad_ref[rr, pl.ds(q0, 2 * Q_BLOCK_A), :]
        vw = vpad_ref[rr, pl.ds(q0, 2 * Q_BLOCK_A), :]
        s = lax.dot_general(q, kw, (((1,), (1,)), ((), ())), preferred_element_type=F32)
        s = s * scale + bias
        key_pos = col + (q0 - half)
        s = jnp.where((key_pos >= 0) & (key_pos < sub_len), s, NEG_INF)
        m = jnp.max(s, axis=-1, keepdims=True)
        p = jnp.exp(s - m)
        l = jnp.sum(p, axis=-1, keepdims=True)
        o = jnp.dot((p / l).astype(BF16), vw, preferred_element_type=F32)
        lse = m + jnp.log(l)
        if dilation == 1:
            rows = pl.ds(q0, Q_BLOCK_A)
        else:
            rows = pl.ds(q0 * dilation + res_blk * n_res + rr, Q_BLOCK_A, stride=dilation)
        o_ref[rows, :] = o
        lse_ref[rows, :] = jnp.where(lane == head, lse, lse_ref[rows, :])

    if n_q >= CHAINS_A:
        def body(it, carry):
            for u in range(CHAINS_A):
                one_block(0, pl.multiple_of((it * CHAINS_A + u) * Q_BLOCK_A, Q_BLOCK_A))
            return carry

        lax.fori_loop(0, n_q // CHAINS_A, body, 0)
    else:
        for rr in range(n_res):
            for i in range(n_q):
                one_block(rr, i * Q_BLOCK_A)


def _alibi_band_bias(group, dilation):
    n = N_GROUPS_A * HEADS_A
    head = jnp.arange(1, n + 1, dtype=F32)
    slopes = jnp.exp2(-8.0 * head / n).reshape(N_GROUPS_A, HEADS_A)[group]
    r = np.arange(Q_BLOCK_A)[:, None]
    c = np.arange(2 * Q_BLOCK_A)[None, :]
    rel = np.abs(c - r - Q_BLOCK_A // 2)
    dist = jnp.asarray(rel * dilation, F32)
    bias = -slopes[:, None, None] * dist[None]
    return jnp.where(jnp.asarray(rel <= Q_BLOCK_A // 2)[None], bias, NEG_INF)


def _window_attn(qkv_g, group, seq_len):
    _, _, n_seq, d, sub_len, _ = qkv_g.shape
    m = n_seq * seq_len
    n_q = sub_len // Q_BLOCK_A
    n_res = 1 if n_q >= CHAINS_A else min(d, CHAINS_A // n_q)

    def in_spec(t):
        return pl.BlockSpec((None, None, None, n_res, sub_len, HEAD_DIM_A), lambda b, h, r: (t, h, b, r, 0, 0))

    return pl.pallas_call(
        functools.partial(_window_attn_kernel, sub_len=sub_len, dilation=d, n_res=n_res),
        grid=(n_seq, HEADS_A, d // n_res),
        in_specs=[in_spec(0), in_spec(1), in_spec(2),
                  pl.BlockSpec((None, Q_BLOCK_A, 2 * Q_BLOCK_A), lambda b, h, r: (h, 0, 0))],
        out_specs=[
            pl.BlockSpec((None, seq_len, HEAD_DIM_A), lambda b, h, r: (h, b, 0)),
            pl.BlockSpec((seq_len, LANES), lambda b, h, r: (b, 0)),
        ],
        out_shape=[
            jax.ShapeDtypeStruct((HEADS_A, m, HEAD_DIM_A), F32),
            jax.ShapeDtypeStruct((m, LANES), F32),
        ],
        scratch_shapes=[
            pltpu.VMEM((n_res, sub_len + Q_BLOCK_A, HEAD_DIM_A), BF16),
            pltpu.VMEM((n_res, sub_len + Q_BLOCK_A, HEAD_DIM_A), BF16),
        ],
        compiler_params=_params("parallel", "arbitrary", "arbitrary"),
        name=f"window_attn_g{group}",
    )(qkv_g, qkv_g, qkv_g, _alibi_band_bias(group, d))


def _mix_out_proj_kernel(x_ref, o0_ref, o1_ref, o2_ref, l0_ref, l1_ref, l2_ref, w_ref, y_ref):
    l0, l1, l2 = l0_ref[...], l1_ref[...], l2_ref[...]
    mx = jnp.maximum(jnp.maximum(l0, l1), l2)
    e0, e1, e2 = jnp.exp(l0 - mx), jnp.exp(l1 - mx), jnp.exp(l2 - mx)
    den = e0 + e1 + e2
    a0, a1, a2 = e0 / den, e1 / den, e2 / den
    shape = (x_ref.shape[0], HEAD_DIM_A)
    cols = []
    for h in range(HEADS_A):
        mixed = (jnp.broadcast_to(a0[:, h:h + 1], shape) * o0_ref[h]
                 + jnp.broadcast_to(a1[:, h:h + 1], shape) * o1_ref[h]
                 + jnp.broadcast_to(a2[:, h:h + 1], shape) * o2_ref[h])
        cols.append(mixed.astype(BF16))
    mixed = jnp.concatenate(cols, axis=-1)
    y_ref[...] = x_ref[...] + jnp.dot(mixed, w_ref[...], preferred_element_type=F32)


def _mix_out_proj(x, outs, lses, w_o, layer, *, tm=256):
    m, d = x.shape
    row = lambda i: (i, 0)
    return pl.pallas_call(
        _mix_out_proj_kernel,
        grid=(m // tm,),
        in_specs=[pl.BlockSpec((tm, d), row)]
        + [pl.BlockSpec((HEADS_A, tm, HEAD_DIM_A), lambda i: (0, i, 0))] * 3
        + [pl.BlockSpec((tm, LANES), row)] * 3
        + [pl.BlockSpec((None, GROUP_WIDTH_A, d), lambda i: (layer, 0, 0))],
        out_specs=pl.BlockSpec((tm, d), row),
        out_shape=jax.ShapeDtypeStruct((m, d), F32),
        compiler_params=_params("parallel"),
        name="mix_out_proj",
    )(x, *outs, *lses, w_o)


def _latent_q_kernel(c_ref, g_ref, w_ref, cos_ref, sin_ref, q_ref):
    h = _rmsnorm(c_ref[...], g_ref[...]).astype(BF16)
    full = jnp.dot(h, w_ref[...], preferred_element_type=F32)
    cos_t, sin_t = cos_ref[...], sin_ref[...]
    per = QK_NOPE_DIM + 2 * LANES
    for hd in range(N_HEADS_B):
        base = hd * per
        nope = full[:, base:base + QK_NOPE_DIM]
        rope = full[:, base + QK_NOPE_DIM:base + QK_NOPE_DIM + LANES]
        swapped = full[:, base + QK_NOPE_DIM + LANES:base + per]
        q_ref[hd, :, 0:QK_NOPE_DIM] = nope.astype(BF16)
        q_ref[hd, :, QK_NOPE_DIM:QK_PAD_B] = (rope * cos_t + swapped * sin_t).astype(BF16)


def _latent_kv_kernel(c_ref, kr_ref, krs_ref, g_ref, w_ref, cos_ref, sin_ref, k_ref, v_ref):
    h = _rmsnorm(c_ref[...], g_ref[...]).astype(BF16)
    kv = jnp.dot(h, w_ref[...], preferred_element_type=F32)
    k_rope = (kr_ref[...] * cos_ref[...] + krs_ref[...] * sin_ref[...]).astype(BF16)
    per = QK_NOPE_DIM + V_HEAD_DIM
    for hd in range(N_HEADS_B):
        k_ref[hd, :, 0:QK_NOPE_DIM] = kv[:, hd * per:hd * per + QK_NOPE_DIM].astype(BF16)
        k_ref[hd, :, QK_NOPE_DIM:QK_PAD_B] = k_rope
        v_ref[hd] = kv[:, hd * per + QK_NOPE_DIM:(hd + 1) * per].astype(BF16)


def _latent_attn_kernel(q_ref, k_ref, v_ref, o_ref, *, n_parts):
    c = (QK_NOPE_DIM + QK_ROPE_DIM) ** -0.5 * math.log2(math.e)
    k = k_ref[...]
    v = v_ref[...]
    rows = q_ref.shape[0] // n_parts
    for part in range(n_parts):
        sl = slice(part * rows, (part + 1) * rows)
        s = lax.dot_general(q_ref[sl, :], k, (((1,), (1,)), ((), ())), preferred_element_type=F32)
        m = jnp.max(s, axis=-1, keepdims=True)
        p = jnp.exp2((s - m) * c)
        l = jnp.sum(p, axis=-1, keepdims=True)
        o = jnp.dot(p.astype(BF16), v, preferred_element_type=F32)
        o_ref[sl, :] = (o / l).astype(o_ref.dtype)


def _rope_tables(seq_len):
    pos = jnp.arange(seq_len, dtype=F32)
    inv_freq = ROPE_THETA ** (-jnp.arange(0, QK_ROPE_DIM, 2, dtype=F32) / QK_ROPE_DIM)
    ang = pos[:, None] * inv_freq[None, :]
    cos, sin = jnp.cos(ang), jnp.sin(ang)
    zeros = jnp.zeros((seq_len, LANES - QK_ROPE_DIM), F32)
    return jnp.concatenate([cos, cos, zeros], axis=-1), jnp.concatenate([-sin, sin, zeros], axis=-1)


def _latent_weights(w_in, w_uq):
    half = QK_ROPE_DIM // 2
    d = w_in.shape[0]
    c_q = w_in[:, :Q_LORA_RANK]
    c_kv = w_in[:, Q_LORA_RANK:Q_LORA_RANK + KV_LORA_RANK]
    x1 = w_in[:, Q_LORA_RANK + KV_LORA_RANK:Q_LORA_RANK + KV_LORA_RANK + half]
    x2 = w_in[:, Q_LORA_RANK + KV_LORA_RANK + half:]
    zpad = jnp.zeros((d, LANES - QK_ROPE_DIM), w_in.dtype)
    w_in_p = jnp.concatenate([c_kv, x1, x2, zpad, x2, x1, zpad, c_q], axis=-1)

    w = w_uq.reshape(Q_LORA_RANK, N_HEADS_B, QK_NOPE_DIM + QK_ROPE_DIM)
    nope = w[:, :, :QK_NOPE_DIM]
    q1 = w[:, :, QK_NOPE_DIM:QK_NOPE_DIM + half]
    q2 = w[:, :, QK_NOPE_DIM + half:]
    zq = jnp.zeros((Q_LORA_RANK, N_HEADS_B, LANES - QK_ROPE_DIM), w_uq.dtype)
    w_uq_p = jnp.concatenate([nope, q1, q2, zq, q2, q1, zq], axis=-1).reshape(Q_LORA_RANK, -1)
    return w_in_p, w_uq_p


def _latent_qkv(c, q_norm, kv_norm, w_uq_p, w_ukv, cos_t, sin_t, layer, seq_len, *, tm=256):
    m = c.shape[0]
    n_pos_blocks = seq_len // tm
    row = lambda i: (i, 0)
    pos = lambda i: (i % n_pos_blocks, 0)
    const = lambda i: (0, 0)
    head_major = lambda i: (0, i, 0)
    q = pl.pallas_call(
        _latent_q_kernel,
        grid=(m // tm,),
        in_specs=[
            pl.BlockSpec((tm, Q_LORA_RANK), lambda i: (i, 1)),
            pl.BlockSpec((None, 1, Q_LORA_RANK), lambda i: (layer, 0, 0)),
            pl.BlockSpec(w_uq_p.shape, const),
            pl.BlockSpec((tm, LANES), pos),
            pl.BlockSpec((tm, LANES), pos),
        ],
        out_specs=pl.BlockSpec((N_HEADS_B, tm, QK_PAD_B), head_major),
        out_shape=jax.ShapeDtypeStruct((N_HEADS_B, m, QK_PAD_B), BF16),
        compiler_params=_params("parallel"),
        name="latent_q",
    )(c, q_norm.reshape(-1, 1, Q_LORA_RANK), w_uq_p, cos_t, sin_t)
    kr_block = KV_LORA_RANK // LANES
    k, v = pl.pallas_call(
        _latent_kv_kernel,
        grid=(m // tm,),
        in_specs=[
            pl.BlockSpec((tm, KV_LORA_RANK), row),
            pl.BlockSpec((tm, LANES), lambda i: (i, kr_block)),
            pl.BlockSpec((tm, LANES), lambda i: (i, kr_block + 1)),
            pl.BlockSpec((None, 1, KV_LORA_RANK), lambda i: (layer, 0, 0)),
            pl.BlockSpec((None,) + w_ukv.shape[1:], lambda i: (layer, 0, 0)),
            pl.BlockSpec((tm, LANES), pos),
            pl.BlockSpec((tm, LANES), pos),
        ],
        out_specs=[
            pl.BlockSpec((N_HEADS_B, tm, QK_PAD_B), head_major),
            pl.BlockSpec((N_HEADS_B, tm, V_HEAD_DIM), head_major),
        ],
        out_shape=[
            jax.ShapeDtypeStruct((N_HEADS_B, m, QK_PAD_B), BF16),
            jax.ShapeDtypeStruct((N_HEADS_B, m, V_HEAD_DIM), BF16),
        ],
        compiler_params=_params("parallel"),
        name="latent_kv",
    )(c, c, c, kv_norm.reshape(-1, 1, KV_LORA_RANK), w_ukv, cos_t, sin_t)
    return q, k, v


def _latent_attn(q, k, v, seq_len, *, tq=512, n_parts=2):
    m = q.shape[1]
    n_seq = m // seq_len
    n_q = seq_len // tq
    return pl.pallas_call(
        functools.partial(_latent_attn_kernel, n_parts=n_parts),
        grid=(n_seq, N_HEADS_B, n_q),
        in_specs=[
            pl.BlockSpec((None, tq, QK_PAD_B), lambda b, h, i: (h, b * n_q + i, 0)),
            pl.BlockSpec((None, seq_len, QK_PAD_B), lambda b, h, i: (h, b, 0)),
            pl.BlockSpec((None, seq_len, V_HEAD_DIM), lambda b, h, i: (h, b, 0)),
        ],
        out_specs=pl.BlockSpec((None, tq, V_HEAD_DIM), lambda b, h, i: (h, b * n_q + i, 0)),
        out_shape=jax.ShapeDtypeStruct((N_HEADS_B, m, V_HEAD_DIM), BF16),
        compiler_params=_params("parallel", "parallel", "arbitrary"),
        name="latent_attn",
    )(q, k, v)


def _out_proj_kernel(x_ref, a_ref, w_ref, y_ref):
    a = jnp.concatenate([a_ref[h] for h in range(a_ref.shape[0])], axis=-1)
    y_ref[...] = x_ref[...] + jnp.dot(a, w_ref[...], preferred_element_type=F32)


def _out_proj(x, a, w, layer, *, tm=256):
    m, d = x.shape
    n_h, _, hd = a.shape
    row = lambda i: (i, 0)
    return pl.pallas_call(
        _out_proj_kernel,
        grid=(m // tm,),
        in_specs=[
            pl.BlockSpec((tm, d), row),
            pl.BlockSpec((n_h, tm, hd), lambda i: (0, i, 0)),
            pl.BlockSpec((None, n_h * hd, d), lambda i: (layer, 0, 0)),
        ],
        out_specs=pl.BlockSpec((tm, d), row),
        out_shape=jax.ShapeDtypeStruct((m, d), F32),
        compiler_params=_params("parallel"),
        name="out_proj",
    )(x, a, w)


def _dilated_mixture_mixer(x, p, i, j, seq_len):
    qkv = _qkv_proj(x, p['mix_norm'], p['a_w_qkv'], i, j, seq_len)
    outs, lses = [], []
    for group in range(N_GROUPS_A):
        o, lse = _window_attn(qkv[group], group, seq_len)
        outs.append(o)
        lses.append(lse)
    return _mix_out_proj(x, outs, lses, p['a_w_o'], j)


def _latent_attention_mixer(x, p, i, j, seq_len):
    w_in_p, w_uq_p = p['b_latent'][j]
    c = _norm_proj(x, p['mix_norm'][i], w_in_p, F32, tm=512, tn=w_in_p.shape[1])
    cos_t, sin_t = _rope_tables(seq_len)
    q, k, v = _latent_qkv(c, p['b_q_norm'], p['b_kv_norm'], w_uq_p, p['b_w_ukv'], cos_t, sin_t, j, seq_len)
    o = _latent_attn(q, k, v, seq_len)
    return _out_proj(x, o, p['b_w_o'], j)


def _trunk(x, seq_len, p):
    depth = p['ffn1_norm'].shape[0]
    for i in range(depth):
        x = _ffn(x, p['ffn1_norm'], p['ffn1_w_gate'], p['ffn1_w_up'], p['ffn1_w_down'], i)
        j = i // 2
        if i % 2 == 0:
            x = _dilated_mixture_mixer(x, p, i, j, seq_len)
        else:
            x = _latent_attention_mixer(x, p, i, j, seq_len)
        final_g = p['final_norm'] if i == depth - 1 else None
        x = _ffn(x, p['ffn2_norm'], p['ffn2_w_gate'], p['ffn2_w_up'], p['ffn2_w_down'], i, final_g)
    return x


def kernel(x_prompt, x_sample, ffn1_norm, ffn1_w_gate, ffn1_w_up, ffn1_w_down, mix_norm, a_w_qkv, a_w_o, b_w_in, b_q_norm, b_w_uq, b_kv_norm, b_w_ukv, b_w_o, ffn2_norm, ffn2_w_gate, ffn2_w_up, ffn2_w_down, final_norm):
    n_b = b_w_in.shape[0]
    latent = [_latent_weights(b_w_in[j], b_w_uq[j]) for j in range(n_b)]
    p = dict(
        ffn1_norm=ffn1_norm, ffn2_norm=ffn2_norm, mix_norm=mix_norm, final_norm=final_norm,
        b_q_norm=b_q_norm, b_kv_norm=b_kv_norm,
        ffn1_w_gate=ffn1_w_gate.astype(BF16), ffn1_w_up=ffn1_w_up.astype(BF16), ffn1_w_down=ffn1_w_down.astype(BF16),
        ffn2_w_gate=ffn2_w_gate.astype(BF16), ffn2_w_up=ffn2_w_up.astype(BF16), ffn2_w_down=ffn2_w_down.astype(BF16),
        a_w_qkv=a_w_qkv.astype(BF16), a_w_o=a_w_o.astype(BF16),
        b_latent=[(w_in_p.astype(BF16), w_uq_p.astype(BF16)) for w_in_p, w_uq_p in latent],
        b_w_ukv=b_w_ukv.astype(BF16), b_w_o=b_w_o.astype(BF16),
    )
    outs = []
    for x in (x_prompt, x_sample):
        b, s, d = x.shape
        outs.append(_trunk(x.reshape(b * s, d), s, p).reshape(b, s, d))
    return tuple(outs)
```

```python
import functools
import math

import numpy as np
import jax
import jax.numpy as jnp
from jax import lax
from jax.experimental import pallas as pl
from jax.experimental.pallas import tpu as pltpu

F32 = jnp.float32
BF16 = jnp.bfloat16

NORM_EPS = 1e-6
NEG_INF = -1e30
LANES = 128

DIL_GROUPS = ((128, 1), (512, 4), (2048, 16))
N_GROUPS_A = 3
HEADS_A = 8
HEAD_DIM_A = 128
GROUP_WIDTH_A = HEADS_A * HEAD_DIM_A
Q_BLOCK_A = 128
CHAINS_A = 8
EDGE_FIRST, EDGE_LAST = 1, 2
N_HEADS_B = 16
Q_LORA_RANK = 768
KV_LORA_RANK = 512
QK_NOPE_DIM = 128
QK_ROPE_DIM = 64
V_HEAD_DIM = 128
ROPE_THETA = 10000.0
QK_PAD_B = 256

VMEM_LIMIT_BYTES = 48 * 1024 * 1024


def _params(*sem):
    return pltpu.CompilerParams(dimension_semantics=sem, vmem_limit_bytes=VMEM_LIMIT_BYTES)


def _rmsnorm(x, g):
    ms = jnp.mean(x * x, axis=-1, keepdims=True)
    return x * lax.rsqrt(ms + NORM_EPS) * g


def _ffn_kernel(*refs, final):
    if final:
        x_ref, g_ref, wg_ref, wu_ref, wd_ref, fg_ref, o_ref, h_ref, acc_ref = refs
    else:
        x_ref, g_ref, wg_ref, wu_ref, wd_ref, o_ref, h_ref, acc_ref = refs
    j = pl.program_id(1)

    @pl.when(j == 0)
    def _():
        h_ref[...] = _rmsnorm(x_ref[...], g_ref[...]).astype(BF16)
        acc_ref[...] = jnp.zeros_like(acc_ref)

    h = h_ref[...]
    gate = jnp.dot(h, wg_ref[...], preferred_element_type=F32)
    up = jnp.dot(h, wu_ref[...], preferred_element_type=F32)
    act = (gate * jax.nn.sigmoid(gate) * up).astype(BF16)
    acc_ref[...] += jnp.dot(act, wd_ref[...], preferred_element_type=F32)

    @pl.when(j == pl.num_programs(1) - 1)
    def _():
        y = x_ref[...] + 0.5 * acc_ref[...]
        if final:
            y = _rmsnorm(y, fg_ref[...])
        o_ref[...] = y


def _ffn(x, g, wg, wu, wd, layer, final_g=None, *, tm=512, tf=512):
    m, d = x.shape
    f = wg.shape[2]
    final = final_g is not None
    in_specs = [
        pl.BlockSpec((tm, d), lambda i, j: (i, 0)),
        pl.BlockSpec((None, 1, d), lambda i, j: (layer, 0, 0)),
        pl.BlockSpec((None, d, tf), lambda i, j: (layer, 0, j)),
        pl.BlockSpec((None, d, tf), lambda i, j: (layer, 0, j)),
        pl.BlockSpec((None, tf, d), lambda i, j: (layer, j, 0)),
    ]
    args = [x, g.reshape(-1, 1, d), wg, wu, wd]
    if final:
        in_specs.append(pl.BlockSpec((1, d), lambda i, j: (0, 0)))
        args.append(final_g.reshape(1, d))
    return pl.pallas_call(
        functools.partial(_ffn_kernel, final=final),
        grid=(m // tm, f // tf),
        in_specs=in_specs,
        out_specs=pl.BlockSpec((tm, d), lambda i, j: (i, 0)),
        out_shape=jax.ShapeDtypeStruct((m, d), F32),
        scratch_shapes=[pltpu.VMEM((tm, d), BF16), pltpu.VMEM((tm, d), F32)],
        compiler_params=_params("parallel", "arbitrary"),
        name="ffn",
    )(*args)


def _norm_proj_kernel(x_ref, g_ref, w_ref, o_ref, h_ref):
    @pl.when(pl.program_id(1) == 0)
    def _():
        h_ref[...] = _rmsnorm(x_ref[...], g_ref[...]).astype(BF16)

    o_ref[...] = jnp.dot(h_ref[...], w_ref[...], preferred_element_type=F32).astype(o_ref.dtype)


def _norm_proj(x, g, w, out_dtype, *, tm, tn):
    m, d = x.shape
    n = w.shape[1]
    return pl.pallas_call(
        _norm_proj_kernel,
        grid=(m // tm, n // tn),
        in_specs=[
            pl.BlockSpec((tm, d), lambda i, j: (i, 0)),
            pl.BlockSpec((1, d), lambda i, j: (0, 0)),
            pl.BlockSpec((d, tn), lambda i, j: (0, j)),
        ],
        out_specs=pl.BlockSpec((tm, tn), lambda i, j: (i, j)),
        out_shape=jax.ShapeDtypeStruct((m, n), out_dtype),
        scratch_shapes=[pltpu.VMEM((tm, d), BF16)],
        compiler_params=_params("parallel", "arbitrary"),
        name="norm_proj",
    )(x, g.reshape(1, d), w)


def _qkv_proj_kernel(x_ref, g_ref, perm_ref, w_ref, o0_ref, o1_ref, o2_ref, hnat_ref, hperm_ref):
    j = pl.program_id(1)

    @pl.when(j == 0)
    def _():
        h = _rmsnorm(x_ref[...], g_ref[...]).astype(BF16)
        hnat_ref[...] = h
        hperm_ref[...] = h

    @pl.when((j == 3) | (j == 6))
    def _():
        hperm_ref[...] = jnp.dot(perm_ref[...], hnat_ref[...], preferred_element_type=F32).astype(BF16)

    res = jnp.dot(hperm_ref[...], w_ref[...], preferred_element_type=F32)
    tm = res.shape[0]
    for group, o_ref in enumerate((o0_ref, o1_ref, o2_ref)):
        dil = DIL_GROUPS[group][1]
        rows = tm // dil

        @pl.when(j // 3 == group)
        def _(o_ref=o_ref, dil=dil, rows=rows):
            for hd in range(HEADS_A):
                for r in range(dil):
                    o_ref[hd, r] = res[r * rows:(r + 1) * rows, hd * HEAD_DIM_A:(hd + 1) * HEAD_DIM_A].astype(BF16)


def _residue_permutations(tm):
    mats = []
    for _, dil in DIL_GROUPS:
        rows = tm // dil
        p = np.zeros((tm, tm), np.float32)
        dst = np.arange(tm)
        src = (dst % rows) * dil + dst // rows
        p[dst, src] = 1.0
        mats.append(p)
    return jnp.asarray(np.stack(mats), BF16)


def _qkv_proj(x, g, w_qkv, g_layer, w_layer, seq_len, *, tm=512):
    m, d_model = x.shape
    n_seq = m // seq_len
    tiles_per_seq = seq_len // tm
    n_col = 3 * N_GROUPS_A

    def out_spec(group):
        dil = DIL_GROUPS[group][1]

        def index(i, j):
            t = jnp.clip(j - 3 * group, 0, 2)
            return (t, 0, i // tiles_per_seq, 0, i % tiles_per_seq, 0)

        return pl.BlockSpec((None, HEADS_A, None, dil, tm // dil, HEAD_DIM_A), index)

    out_shape = [
        jax.ShapeDtypeStruct((3, HEADS_A, n_seq, dil, seq_len // dil, HEAD_DIM_A), BF16) for _, dil in DIL_GROUPS
    ]
    return pl.pallas_call(
        _qkv_proj_kernel,
        grid=(m // tm, n_col),
        in_specs=[
            pl.BlockSpec((tm, d_model), lambda i, j: (i, 0)),
            pl.BlockSpec((None, 1, d_model), lambda i, j: (g_layer, 0, 0)),
            pl.BlockSpec((None, tm, tm), lambda i, j: (j // 3, 0, 0)),
            pl.BlockSpec((None, d_model, GROUP_WIDTH_A), lambda i, j: (w_layer, 0, (j % 3) * N_GROUPS_A + j // 3)),
        ],
        out_specs=[out_spec(group) for group in range(N_GROUPS_A)],
        out_shape=out_shape,
        scratch_shapes=[pltpu.VMEM((tm, d_model), BF16), pltpu.VMEM((tm, d_model), BF16)],
        compiler_params=_params("parallel", "arbitrary"),
        name="qkv_proj",
    )(x, g.reshape(-1, 1, d_model), _residue_permutations(tm), w_qkv)


def _window_attn_kernel(q_ref, k_ref, v_ref, bias_ref, o_ref, lse_ref, kpad_ref, vpad_ref, *,
                        sub_len, dilation, n_res):
    half = Q_BLOCK_A // 2
    res_blk = pl.program_id(2)
    n_q = sub_len // Q_BLOCK_A
    scale = HEAD_DIM_A ** -0.5
    c = scale * math.log2(math.e)

    zeros = jnp.zeros((half, HEAD_DIM_A), BF16)
    for rr in range(n_res):
        kpad_ref[rr, 0:half, :] = zeros
        kpad_ref[rr, half + sub_len:2 * half + sub_len, :] = zeros
        kpad_ref[rr, half:half + sub_len, :] = k_ref[rr]
        vpad_ref[rr, 0:half, :] = zeros
        vpad_ref[rr, half + sub_len:2 * half + sub_len, :] = zeros
        vpad_ref[rr, half:half + sub_len, :] = v_ref[rr]

    def one_block(rr, q0, edge):
        q = q_ref[rr, pl.ds(q0, Q_BLOCK_A), :]
        kw = kpad_ref[rr, pl.ds(q0, 2 * Q_BLOCK_A), :]
        vw = vpad_ref[rr, pl.ds(q0, 2 * Q_BLOCK_A), :]
        t = lax.dot_general(q, kw, (((1,), (1,)), ((), ())), preferred_element_type=F32) + bias_ref[edge]
        m = jnp.max(t, axis=-1, keepdims=True)
        p = jnp.exp2((t - m) * c)
        l = jnp.sum(p, axis=-1, keepdims=True)
        o = jnp.dot(p.astype(BF16), vw, preferred_element_type=F32) / l
        lse = m * scale + jnp.log(l)
        if dilation == 1:
            rows = pl.ds(q0, Q_BLOCK_A)
        else:
            rows = pl.ds(q0 * dilation + res_blk * n_res + rr, Q_BLOCK_A, stride=dilation)
        o_ref[rows, :] = o
        lse_ref[rows, :] = jnp.broadcast_to(lse, (Q_BLOCK_A, LANES))

    if n_q >= CHAINS_A:
        n_it = n_q // CHAINS_A

        def body(it, carry):
            for u in range(CHAINS_A):
                if u == 0:
                    edge = jnp.where(it == 0, EDGE_FIRST, 0)
                elif u == CHAINS_A - 1:
                    edge = jnp.where(it == n_it - 1, EDGE_LAST, 0)
                else:
                    edge = 0
                one_block(0, pl.multiple_of((it * CHAINS_A + u) * Q_BLOCK_A, Q_BLOCK_A), edge)
            return carry

        lax.fori_loop(0, n_it, body, 0)
    else:
        for rr in range(n_res):
            for i in range(n_q):
                one_block(rr, i * Q_BLOCK_A, (EDGE_FIRST if i == 0 else 0) | (EDGE_LAST if i == n_q - 1 else 0))


def _alibi_band_bias(group, dilation):
    n = N_GROUPS_A * HEADS_A
    head = jnp.arange(1, n + 1, dtype=F32)
    slopes = jnp.exp2(-8.0 * head / n).reshape(N_GROUPS_A, HEADS_A)[group]
    half = Q_BLOCK_A // 2
    r = np.arange(Q_BLOCK_A)[:, None]
    c = np.arange(2 * Q_BLOCK_A)[None, :]
    rel = np.abs(c - r - half)
    valid = []
    for edge in range(4):
        v = rel <= half
        if edge & EDGE_FIRST:
            v = v & (c >= half)
        if edge & EDGE_LAST:
            v = v & (c < Q_BLOCK_A + half)
        valid.append(v)
    valid = jnp.asarray(np.stack(valid))
    bias = -slopes[:, None, None, None] * jnp.asarray(rel * dilation, F32)[None, None]
    return jnp.where(valid[None], bias, NEG_INF) * (HEAD_DIM_A ** 0.5)


def _window_attn(qkv_g, group, seq_len):
    _, _, n_seq, d, sub_len, _ = qkv_g.shape
    m = n_seq * seq_len
    n_q = sub_len // Q_BLOCK_A
    n_res = 1 if n_q >= CHAINS_A else min(d, CHAINS_A // n_q)

    def in_spec(t):
        return pl.BlockSpec((None, None, None, n_res, sub_len, HEAD_DIM_A), lambda b, h, r: (t, h, b, r, 0, 0))

    out_spec = pl.BlockSpec((None, seq_len, HEAD_DIM_A), lambda b, h, r: (h, b, 0))
    out_shape = jax.ShapeDtypeStruct((HEADS_A, m, HEAD_DIM_A), F32)
    return pl.pallas_call(
        functools.partial(_window_attn_kernel, sub_len=sub_len, dilation=d, n_res=n_res),
        grid=(n_seq, HEADS_A, d // n_res),
        in_specs=[in_spec(0), in_spec(1), in_spec(2),
                  pl.BlockSpec((None, 4, Q_BLOCK_A, 2 * Q_BLOCK_A), lambda b, h, r: (h, 0, 0, 0))],
        out_specs=[out_spec, out_spec],
        out_shape=[out_shape, out_shape],
        scratch_shapes=[
            pltpu.VMEM((n_res, sub_len + Q_BLOCK_A, HEAD_DIM_A), BF16),
            pltpu.VMEM((n_res, sub_len + Q_BLOCK_A, HEAD_DIM_A), BF16),
        ],
        compiler_params=_params("parallel", "parallel", "arbitrary"),
        name=f"window_attn_g{group}",
    )(qkv_g, qkv_g, qkv_g, _alibi_band_bias(group, d))


def _mix_out_proj_kernel(x_ref, o0_ref, o1_ref, o2_ref, l0_ref, l1_ref, l2_ref, w_ref, y_ref):
    cols = []
    for h in range(HEADS_A):
        l0, l1, l2 = l0_ref[h], l1_ref[h], l2_ref[h]
        mx = jnp.maximum(jnp.maximum(l0, l1), l2)
        e0, e1, e2 = jnp.exp(l0 - mx), jnp.exp(l1 - mx), jnp.exp(l2 - mx)
        mixed = (e0 * o0_ref[h] + e1 * o1_ref[h] + e2 * o2_ref[h]) / (e0 + e1 + e2)
        cols.append(mixed.astype(BF16))
    mixed = jnp.concatenate(cols, axis=-1)
    y_ref[...] = x_ref[...] + jnp.dot(mixed, w_ref[...], preferred_element_type=F32)


def _mix_out_proj(x, outs, lses, w_o, layer, *, tm=256):
    m, d = x.shape
    row = lambda i: (i, 0)
    return pl.pallas_call(
        _mix_out_proj_kernel,
        grid=(m // tm,),
        in_specs=[pl.BlockSpec((tm, d), row)]
        + [pl.BlockSpec((HEADS_A, tm, HEAD_DIM_A), lambda i: (0, i, 0))] * 6
        + [pl.BlockSpec((None, GROUP_WIDTH_A, d), lambda i: (layer, 0, 0))],
        out_specs=pl.BlockSpec((tm, d), row),
        out_shape=jax.ShapeDtypeStruct((m, d), F32),
        compiler_params=_params("parallel"),
        name="mix_out_proj",
    )(x, *outs, *lses, w_o)


def _latent_q_kernel(c_ref, g_ref, w_ref, cos_ref, sin_ref, q_ref):
    h = _rmsnorm(c_ref[...], g_ref[...]).astype(BF16)
    full = jnp.dot(h, w_ref[...], preferred_element_type=F32)
    cos_t, sin_t = cos_ref[...], sin_ref[...]
    per = QK_NOPE_DIM + 2 * LANES
    for hd in range(N_HEADS_B):
        base = hd * per
        nope = full[:, base:base + QK_NOPE_DIM]
        rope = full[:, base + QK_NOPE_DIM:base + QK_NOPE_DIM + LANES]
        swapped = full[:, base + QK_NOPE_DIM + LANES:base + per]
        q_ref[hd, :, 0:QK_NOPE_DIM] = nope.astype(BF16)
        q_ref[hd, :, QK_NOPE_DIM:QK_PAD_B] = (rope * cos_t + swapped * sin_t).astype(BF16)


def _latent_kv_kernel(c_ref, kr_ref, krs_ref, g_ref, w_ref, cos_ref, sin_ref, k_ref, v_ref):
    h = _rmsnorm(c_ref[...], g_ref[...]).astype(BF16)
    kv = jnp.dot(h, w_ref[...], preferred_element_type=F32)
    k_rope = (kr_ref[...] * cos_ref[...] + krs_ref[...] * sin_ref[...]).astype(BF16)
    per = QK_NOPE_DIM + V_HEAD_DIM
    for hd in range(N_HEADS_B):
        k_ref[hd, :, 0:QK_NOPE_DIM] = kv[:, hd * per:hd * per + QK_NOPE_DIM].astype(BF16)
        k_ref[hd, :, QK_NOPE_DIM:QK_PAD_B] = k_rope
        v_ref[hd] = kv[:, hd * per + QK_NOPE_DIM:(hd + 1) * per].astype(BF16)


def _latent_attn_kernel(q_ref, k_ref, v_ref, o_ref, *, n_parts):
    c = (QK_NOPE_DIM + QK_ROPE_DIM) ** -0.5 * math.log2(math.e)
    k = k_ref[...]
    v = v_ref[...]
    rows = q_ref.shape[0] // n_parts
    for part in range(n_parts):
        sl = slice(part * rows, (part + 1) * rows)
        s = lax.dot_general(q_ref[sl, :], k, (((1,), (1,)), ((), ())), preferred_element_type=F32)
        m = jnp.max(s, axis=-1, keepdims=True)
        p = jnp.exp2((s - m) * c)
        l = jnp.sum(p, axis=-1, keepdims=True)
        o = jnp.dot(p.astype(BF16), v, preferred_element_type=F32)
        o_ref[sl, :] = (o / l).astype(o_ref.dtype)


def _rope_tables(seq_len):
    pos = jnp.arange(seq_len, dtype=F32)
    inv_freq = ROPE_THETA ** (-jnp.arange(0, QK_ROPE_DIM, 2, dtype=F32) / QK_ROPE_DIM)
    ang = pos[:, None] * inv_freq[None, :]
    cos, sin = jnp.cos(ang), jnp.sin(ang)
    zeros = jnp.zeros((seq_len, LANES - QK_ROPE_DIM), F32)
    return jnp.concatenate([cos, cos, zeros], axis=-1), jnp.concatenate([-sin, sin, zeros], axis=-1)


def _latent_weights(w_in, w_uq):
    half = QK_ROPE_DIM // 2
    d = w_in.shape[0]
    c_q = w_in[:, :Q_LORA_RANK]
    c_kv = w_in[:, Q_LORA_RANK:Q_LORA_RANK + KV_LORA_RANK]
    x1 = w_in[:, Q_LORA_RANK + KV_LORA_RANK:Q_LORA_RANK + KV_LORA_RANK + half]
    x2 = w_in[:, Q_LORA_RANK + KV_LORA_RANK + half:]
    zpad = jnp.zeros((d, LANES - QK_ROPE_DIM), w_in.dtype)
    w_in_p = jnp.concatenate([c_kv, x1, x2, zpad, x2, x1, zpad, c_q], axis=-1)

    w = w_uq.reshape(Q_LORA_RANK, N_HEADS_B, QK_NOPE_DIM + QK_ROPE_DIM)
    nope = w[:, :, :QK_NOPE_DIM]
    q1 = w[:, :, QK_NOPE_DIM:QK_NOPE_DIM + half]
    q2 = w[:, :, QK_NOPE_DIM + half:]
    zq = jnp.zeros((Q_LORA_RANK, N_HEADS_B, LANES - QK_ROPE_DIM), w_uq.dtype)
    w_uq_p = jnp.concatenate([nope, q1, q2, zq, q2, q1, zq], axis=-1).reshape(Q_LORA_RANK, -1)
    return w_in_p, w_uq_p


def _latent_qkv(c, q_norm, kv_norm, w_uq_p, w_ukv, cos_t, sin_t, layer, seq_len, *, tm=256):
    m = c.shape[0]
    n_pos_blocks = seq_len // tm
    row = lambda i: (i, 0)
    pos = lambda i: (i % n_pos_blocks, 0)
    const = lambda i: (0, 0)
    head_major = lambda i: (0, i, 0)
    q = pl.pallas_call(
        _latent_q_kernel,
        grid=(m // tm,),
        in_specs=[
            pl.BlockSpec((tm, Q_LORA_RANK), lambda i: (i, 1)),
            pl.BlockSpec((None, 1, Q_LORA_RANK), lambda i: (layer, 0, 0)),
            pl.BlockSpec(w_uq_p.shape, const),
            pl.BlockSpec((tm, LANES), pos),
            pl.BlockSpec((tm, LANES), pos),
        ],
        out_specs=pl.BlockSpec((N_HEADS_B, tm, QK_PAD_B), head_major),
        out_shape=jax.ShapeDtypeStruct((N_HEADS_B, m, QK_PAD_B), BF16),
        compiler_params=_params("parallel"),
        name="latent_q",
    )(c, q_norm.reshape(-1, 1, Q_LORA_RANK), w_uq_p, cos_t, sin_t)
    kr_block = KV_LORA_RANK // LANES
    k, v = pl.pallas_call(
        _latent_kv_kernel,
        grid=(m // tm,),
        in_specs=[
            pl.BlockSpec((tm, KV_LORA_RANK), row),
            pl.BlockSpec((tm, LANES), lambda i: (i, kr_block)),
            pl.BlockSpec((tm, LANES), lambda i: (i, kr_block + 1)),
            pl.BlockSpec((None, 1, KV_LORA_RANK), lambda i: (layer, 0, 0)),
            pl.BlockSpec((None,) + w_ukv.shape[1:], lambda i: (layer, 0, 0)),
            pl.BlockSpec((tm, LANES), pos),
            pl.BlockSpec((tm, LANES), pos),
        ],
        out_specs=[
            pl.BlockSpec((N_HEADS_B, tm, QK_PAD_B), head_major),
            pl.BlockSpec((N_HEADS_B, tm, V_HEAD_DIM), head_major),
        ],
        out_shape=[
            jax.ShapeDtypeStruct((N_HEADS_B, m, QK_PAD_B), BF16),
            jax.ShapeDtypeStruct((N_HEADS_B, m, V_HEAD_DIM), BF16),
        ],
        compiler_params=_params("parallel"),
        name="latent_kv",
    )(c, c, c, kv_norm.reshape(-1, 1, KV_LORA_RANK), w_ukv, cos_t, sin_t)
    return q, k, v


def _latent_attn(q, k, v, seq_len, *, part_rows=256):
    m = q.shape[1]
    n_seq = m // seq_len
    tq = min(seq_len, 2048)
    n_q = seq_len // tq
    return pl.pallas_call(
        functools.partial(_latent_attn_kernel, n_parts=tq // part_rows),
        grid=(n_seq, N_HEADS_B, n_q),
        in_specs=[
            pl.BlockSpec((None, tq, QK_PAD_B), lambda b, h, i: (h, b * n_q + i, 0)),
            pl.BlockSpec((None, seq_len, QK_PAD_B), lambda b, h, i: (h, b, 0)),
            pl.BlockSpec((None, seq_len, V_HEAD_DIM), lambda b, h, i: (h, b, 0)),
        ],
        out_specs=pl.BlockSpec((None, tq, V_HEAD_DIM), lambda b, h, i: (h, b * n_q + i, 0)),
        out_shape=jax.ShapeDtypeStruct((N_HEADS_B, m, V_HEAD_DIM), BF16),
        compiler_params=_params("parallel", "parallel", "arbitrary"),
        name="latent_attn",
    )(q, k, v)


def _out_proj_kernel(x_ref, a_ref, w_ref, y_ref):
    a = jnp.concatenate([a_ref[h] for h in range(a_ref.shape[0])], axis=-1)
    y_ref[...] = x_ref[...] + jnp.dot(a, w_ref[...], preferred_element_type=F32)


def _out_proj(x, a, w, layer, *, tm=256):
    m, d = x.shape
    n_h, _, hd = a.shape
    row = lambda i: (i, 0)
    return pl.pallas_call(
        _out_proj_kernel,
        grid=(m // tm,),
        in_specs=[
            pl.BlockSpec((tm, d), row),
            pl.BlockSpec((n_h, tm, hd), lambda i: (0, i, 0)),
            pl.BlockSpec((None, n_h * hd, d), lambda i: (layer, 0, 0)),
        ],
        out_specs=pl.BlockSpec((tm, d), row),
        out_shape=jax.ShapeDtypeStruct((m, d), F32),
        compiler_params=_params("parallel"),
        name="out_proj",
    )(x, a, w)


def _dilated_mixture_mixer(x, p, i, j, seq_len):
    qkv = _qkv_proj(x, p['mix_norm'], p['a_w_qkv'], i, j, seq_len)
    outs, lses = [], []
    for group in range(N_GROUPS_A):
        o, lse = _window_attn(qkv[group], group, seq_len)
        outs.append(o)
        lses.append(lse)
    return _mix_out_proj(x, outs, lses, p['a_w_o'], j)


def _latent_attention_mixer(x, p, i, j, seq_len):
    w_in_p, w_uq_p = p['b_latent'][j]
    c = _norm_proj(x, p['mix_norm'][i], w_in_p, F32, tm=512, tn=w_in_p.shape[1])
    cos_t, sin_t = _rope_tables(seq_len)
    q, k, v = _latent_qkv(c, p['b_q_norm'], p['b_kv_norm'], w_uq_p, p['b_w_ukv'], cos_t, sin_t, j, seq_len)
    o = _latent_attn(q, k, v, seq_len)
    return _out_proj(x, o, p['b_w_o'], j)


def _trunk(x, seq_len, p):
    depth = p['ffn1_norm'].shape[0]
    for i in range(depth):
        x = _ffn(x, p['ffn1_norm'], p['ffn1_w_gate'], p['ffn1_w_up'], p['ffn1_w_down'], i)
        j = i // 2
        if i % 2 == 0:
            x = _dilated_mixture_mixer(x, p, i, j, seq_len)
        else:
            x = _latent_attention_mixer(x, p, i, j, seq_len)
        final_g = p['final_norm'] if i == depth - 1 else None
        x = _ffn(x, p['ffn2_norm'], p['ffn2_w_gate'], p['ffn2_w_up'], p['ffn2_w_down'], i, final_g)
    return x


def kernel(x_prompt, x_sample, ffn1_norm, ffn1_w_gate, ffn1_w_up, ffn1_w_down, mix_norm, a_w_qkv, a_w_o, b_w_in, b_q_norm, b_w_uq, b_kv_norm, b_w_ukv, b_w_o, ffn2_norm, ffn2_w_gate, ffn2_w_up, ffn2_w_down, final_norm):
    n_b = b_w_in.shape[0]
    latent = [_latent_weights(b_w_in[j], b_w_uq[j]) for j in range(n_b)]
    p = dict(
        ffn1_norm=ffn1_norm, ffn2_norm=ffn2_norm, mix_norm=mix_norm, final_norm=final_norm,
        b_q_norm=b_q_norm, b_kv_norm=b_kv_norm,
        ffn1_w_gate=ffn1_w_gate.astype(BF16), ffn1_w_up=ffn1_w_up.astype(BF16), ffn1_w_down=ffn1_w_down.astype(BF16),
        ffn2_w_gate=ffn2_w_gate.astype(BF16), ffn2_w_up=ffn2_w_up.astype(BF16), ffn2_w_down=ffn2_w_down.astype(BF16),
        a_w_qkv=a_w_qkv.astype(BF16), a_w_o=a_w_o.astype(BF16),
        b_latent=[(w_in_p.astype(BF16), w_uq_p.astype(BF16)) for w_in_p, w_uq_p in latent],
        b_w_ukv=b_w_ukv.astype(BF16), b_w_o=b_w_o.astype(BF16),
    )
    outs = []
    for x in (x_prompt, x_sample):
        b, s, d = x.shape
        outs.append(_trunk(x.reshape(b * s, d), s, p).reshape(b, s, d))
    return tuple(outs)
```

```python
import functools
import math

import numpy as np
import jax
import jax.numpy as jnp
from jax import lax
from jax.experimental import pallas as pl
from jax.experimental.pallas import tpu as pltpu

F32 = jnp.float32
BF16 = jnp.bfloat16

NORM_EPS = 1e-6
NEG_INF = -1e30
NORM_CHUNK_ROWS = 256
LANES = 128

DIL_GROUPS = ((128, 1), (512, 4), (2048, 16))
N_GROUPS_A = 3
HEADS_A = 8
HEAD_DIM_A = 128
GROUP_WIDTH_A = HEADS_A * HEAD_DIM_A
Q_BLOCK_A = 128
PERM_ROWS_A = 512
CHAINS_A = 8
EDGE_FIRST, EDGE_LAST = 1, 2
N_HEADS_B = 16
Q_LORA_RANK = 768
KV_LORA_RANK = 512
QK_NOPE_DIM = 128
QK_ROPE_DIM = 64
V_HEAD_DIM = 128
ROPE_THETA = 10000.0
QK_PAD_B = 256

VMEM_LIMIT_BYTES = 48 * 1024 * 1024
VMEM_LIMIT_LARGE_BYTES = 58 * 1024 * 1024


def _params(*sem, vmem_limit_bytes=VMEM_LIMIT_BYTES):
    return pltpu.CompilerParams(dimension_semantics=sem, vmem_limit_bytes=vmem_limit_bytes)


def _rmsnorm(x, g):
    ms = jnp.mean(x * x, axis=-1, keepdims=True)
    return x * lax.rsqrt(ms + NORM_EPS) * g


def _ffn_kernel(*refs, final):
    if final:
        x_ref, g_ref, wg_ref, wu_ref, wd_ref, fg_ref, o_ref, h_ref = refs
    else:
        x_ref, g_ref, wg_ref, wu_ref, wd_ref, o_ref, h_ref = refs
    j = pl.program_id(1)

    tm = x_ref.shape[0]
    chunks = [slice(r, r + NORM_CHUNK_ROWS) for r in range(0, tm, NORM_CHUNK_ROWS)]

    @pl.when(j == 0)
    def _():
        for rows in chunks:
            h_ref[rows, :] = _rmsnorm(x_ref[rows, :], g_ref[...]).astype(BF16)
        o_ref[...] = jnp.zeros_like(o_ref)

    h = h_ref[...]
    gate = jnp.dot(h, wg_ref[...], preferred_element_type=F32)
    up = jnp.dot(h, wu_ref[...], preferred_element_type=F32)
    act = (gate * jax.nn.sigmoid(gate) * up).astype(BF16)
    o_ref[...] += jnp.dot(act, wd_ref[...], preferred_element_type=F32)

    @pl.when(j == pl.num_programs(1) - 1)
    def _():
        for rows in chunks:
            y = x_ref[rows, :] + 0.5 * o_ref[rows, :]
            if final:
                y = _rmsnorm(y, fg_ref[...])
            o_ref[rows, :] = y


def _ffn(x, g, wg, wu, wd, layer, final_g=None, *, tf=512):
    m, d = x.shape
    f = wg.shape[2]
    final = final_g is not None
    tm = 512 if final else 1024
    in_specs = [
        pl.BlockSpec((tm, d), lambda i, j: (i, 0)),
        pl.BlockSpec((None, 1, d), lambda i, j: (layer, 0, 0)),
        pl.BlockSpec((None, d, tf), lambda i, j: (layer, 0, j)),
        pl.BlockSpec((None, d, tf), lambda i, j: (layer, 0, j)),
        pl.BlockSpec((None, tf, d), lambda i, j: (layer, j, 0)),
    ]
    args = [x, g.reshape(-1, 1, d), wg, wu, wd]
    if final:
        in_specs.append(pl.BlockSpec((1, d), lambda i, j: (0, 0)))
        args.append(final_g.reshape(1, d))
    return pl.pallas_call(
        functools.partial(_ffn_kernel, final=final),
        grid=(m // tm, f // tf),
        in_specs=in_specs,
        out_specs=pl.BlockSpec((tm, d), lambda i, j: (i, 0)),
        out_shape=jax.ShapeDtypeStruct((m, d), F32),
        scratch_shapes=[pltpu.VMEM((tm, d), BF16)],
        compiler_params=_params("parallel", "arbitrary", vmem_limit_bytes=VMEM_LIMIT_LARGE_BYTES),
        name="ffn",
    )(*args)


def _norm_proj_kernel(x_ref, g_ref, w_ref, o_ref, h_ref):
    @pl.when(pl.program_id(1) == 0)
    def _():
        h_ref[...] = _rmsnorm(x_ref[...], g_ref[...]).astype(BF16)

    o_ref[...] = jnp.dot(h_ref[...], w_ref[...], preferred_element_type=F32).astype(o_ref.dtype)


def _norm_proj(x, g, w, out_dtype, *, tm, tn):
    m, d = x.shape
    n = w.shape[1]
    return pl.pallas_call(
        _norm_proj_kernel,
        grid=(m // tm, n // tn),
        in_specs=[
            pl.BlockSpec((tm, d), lambda i, j: (i, 0)),
            pl.BlockSpec((1, d), lambda i, j: (0, 0)),
            pl.BlockSpec((d, tn), lambda i, j: (0, j)),
        ],
        out_specs=pl.BlockSpec((tm, tn), lambda i, j: (i, j)),
        out_shape=jax.ShapeDtypeStruct((m, n), out_dtype),
        scratch_shapes=[pltpu.VMEM((tm, d), BF16)],
        compiler_params=_params("parallel", "arbitrary"),
        name="norm_proj",
    )(x, g.reshape(1, d), w)


def _qkv_proj_kernel(x_ref, g_ref, perm_ref, w_ref, o0_ref, o1_ref, o2_ref, hnat_ref, hperm_ref):
    j = pl.program_id(1)

    @pl.when(j == 0)
    def _():
        h = _rmsnorm(x_ref[...], g_ref[...]).astype(BF16)
        hnat_ref[...] = h
        hperm_ref[...] = h

    tm = x_ref.shape[0]
    n_sub = tm // PERM_ROWS_A

    @pl.when((j == 3) | (j == 6))
    def _():
        for sub in range(n_sub):
            sl = slice(sub * PERM_ROWS_A, (sub + 1) * PERM_ROWS_A)
            hperm_ref[sl, :] = jnp.dot(perm_ref[...], hnat_ref[sl, :], preferred_element_type=F32).astype(BF16)

    res = jnp.dot(hperm_ref[...], w_ref[...], preferred_element_type=F32)
    for group, o_ref in enumerate((o0_ref, o1_ref, o2_ref)):
        dil = DIL_GROUPS[group][1]
        rows = PERM_ROWS_A // dil

        @pl.when(j // 3 == group)
        def _(o_ref=o_ref, dil=dil, rows=rows):
            for sub in range(n_sub):
                for hd in range(HEADS_A):
                    for r in range(dil):
                        src = sub * PERM_ROWS_A + r * rows
                        o_ref[hd, r, sub * rows:(sub + 1) * rows, :] = (
                            res[src:src + rows, hd * HEAD_DIM_A:(hd + 1) * HEAD_DIM_A].astype(BF16))


def _residue_permutations():
    tm = PERM_ROWS_A
    mats = []
    for _, dil in DIL_GROUPS:
        rows = tm // dil
        p = np.zeros((tm, tm), np.float32)
        dst = np.arange(tm)
        src = (dst % rows) * dil + dst // rows
        p[dst, src] = 1.0
        mats.append(p)
    return jnp.asarray(np.stack(mats), BF16)


def _qkv_proj(x, g, w_qkv, g_layer, w_layer, seq_len, *, tm=1024):
    m, d_model = x.shape
    n_seq = m // seq_len
    tiles_per_seq = seq_len // tm
    n_col = 3 * N_GROUPS_A

    def out_spec(group):
        dil = DIL_GROUPS[group][1]

        def index(i, j):
            t = jnp.clip(j - 3 * group, 0, 2)
            return (t, 0, i // tiles_per_seq, 0, i % tiles_per_seq, 0)

        return pl.BlockSpec((None, HEADS_A, None, dil, tm // dil, HEAD_DIM_A), index)

    out_shape = [
        jax.ShapeDtypeStruct((3, HEADS_A, n_seq, dil, seq_len // dil, HEAD_DIM_A), BF16) for _, dil in DIL_GROUPS
    ]
    return pl.pallas_call(
        _qkv_proj_kernel,
        grid=(m // tm, n_col),
        in_specs=[
            pl.BlockSpec((tm, d_model), lambda i, j: (i, 0)),
            pl.BlockSpec((None, 1, d_model), lambda i, j: (g_layer, 0, 0)),
            pl.BlockSpec((None, PERM_ROWS_A, PERM_ROWS_A), lambda i, j: (j // 3, 0, 0)),
            pl.BlockSpec((None, d_model, GROUP_WIDTH_A), lambda i, j: (w_layer, 0, (j % 3) * N_GROUPS_A + j // 3)),
        ],
        out_specs=[out_spec(group) for group in range(N_GROUPS_A)],
        out_shape=out_shape,
        scratch_shapes=[pltpu.VMEM((tm, d_model), BF16), pltpu.VMEM((tm, d_model), BF16)],
        compiler_params=_params("parallel", "arbitrary", vmem_limit_bytes=VMEM_LIMIT_LARGE_BYTES),
        name="qkv_proj",
    )(x, g.reshape(-1, 1, d_model), _residue_permutations(), w_qkv)


def _window_attn_kernel(q_ref, k_ref, v_ref, bias_ref, o_ref, lse_ref, kpad_ref, vpad_ref, *,
                        sub_len, dilation, n_res):
    half = Q_BLOCK_A // 2
    res_blk = pl.program_id(2)
    n_q = sub_len // Q_BLOCK_A
    scale = HEAD_DIM_A ** -0.5
    c = scale * math.log2(math.e)

    zeros = jnp.zeros((half, HEAD_DIM_A), BF16)
    for rr in range(n_res):
        kpad_ref[rr, 0:half, :] = zeros
        kpad_ref[rr, half + sub_len:2 * half + sub_len, :] = zeros
        kpad_ref[rr, half:half + sub_len, :] = k_ref[rr]
        vpad_ref[rr, 0:half, :] = zeros
        vpad_ref[rr, half + sub_len:2 * half + sub_len, :] = zeros
        vpad_ref[rr, half:half + sub_len, :] = v_ref[rr]

    def one_block(rr, q0, edge):
        q = q_ref[rr, pl.ds(q0, Q_BLOCK_A), :]
        kw = kpad_ref[rr, pl.ds(q0, 2 * Q_BLOCK_A), :]
        vw = vpad_ref[rr, pl.ds(q0, 2 * Q_BLOCK_A), :]
        t = lax.dot_general(q, kw, (((1,), (1,)), ((), ())), preferred_element_type=F32) + bias_ref[edge]
        m = jnp.max(t, axis=-1, keepdims=True)
        p = jnp.exp2((t - m) * c)
        l = jnp.sum(p, axis=-1, keepdims=True)
        o = jnp.dot(p.astype(BF16), vw, preferred_element_type=F32) / l
        lse = m * scale + jnp.log(l)
        if dilation == 1:
            rows = pl.ds(q0, Q_BLOCK_A)
        else:
            rows = pl.ds(q0 * dilation + res_blk * n_res + rr, Q_BLOCK_A, stride=dilation)
        o_ref[rows, :] = o
        lse_ref[rows, :] = jnp.broadcast_to(lse, (Q_BLOCK_A, LANES))

    if n_q >= CHAINS_A:
        n_it = n_q // CHAINS_A

        def body(it, carry):
            for u in range(CHAINS_A):
                if u == 0:
                    edge = jnp.where(it == 0, EDGE_FIRST, 0)
                elif u == CHAINS_A - 1:
                    edge = jnp.where(it == n_it - 1, EDGE_LAST, 0)
                else:
                    edge = 0
                one_block(0, pl.multiple_of((it * CHAINS_A + u) * Q_BLOCK_A, Q_BLOCK_A), edge)
            return carry

        lax.fori_loop(0, n_it, body, 0)
    else:
        for rr in range(n_res):
            for i in range(n_q):
                one_block(rr, i * Q_BLOCK_A, (EDGE_FIRST if i == 0 else 0) | (EDGE_LAST if i == n_q - 1 else 0))


def _alibi_band_bias(group, dilation):
    n = N_GROUPS_A * HEADS_A
    head = jnp.arange(1, n + 1, dtype=F32)
    slopes = jnp.exp2(-8.0 * head / n).reshape(N_GROUPS_A, HEADS_A)[group]
    half = Q_BLOCK_A // 2
    r = np.arange(Q_BLOCK_A)[:, None]
    c = np.arange(2 * Q_BLOCK_A)[None, :]
    rel = np.abs(c - r - half)
    valid = []
    for edge in range(4):
        v = rel <= half
        if edge & EDGE_FIRST:
            v = v & (c >= half)
        if edge & EDGE_LAST:
            v = v & (c < Q_BLOCK_A + half)
        valid.append(v)
    valid = jnp.asarray(np.stack(valid))
    bias = -slopes[:, None, None, None] * jnp.asarray(rel * dilation, F32)[None, None]
    return jnp.where(valid[None], bias, NEG_INF) * (HEAD_DIM_A ** 0.5)


def _window_attn(qkv_g, group, seq_len):
    _, _, n_seq, d, sub_len, _ = qkv_g.shape
    m = n_seq * seq_len
    n_q = sub_len // Q_BLOCK_A
    n_res = 1 if n_q >= CHAINS_A else min(d, CHAINS_A // n_q)

    def in_spec(t):
        return pl.BlockSpec((None, None, None, n_res, sub_len, HEAD_DIM_A), lambda b, h, r: (t, h, b, r, 0, 0))

    out_spec = pl.BlockSpec((None, seq_len, HEAD_DIM_A), lambda b, h, r: (h, b, 0))
    out_shape = jax.ShapeDtypeStruct((HEADS_A, m, HEAD_DIM_A), F32)
    return pl.pallas_call(
        functools.partial(_window_attn_kernel, sub_len=sub_len, dilation=d, n_res=n_res),
        grid=(n_seq, HEADS_A, d // n_res),
        in_specs=[in_spec(0), in_spec(1), in_spec(2),
                  pl.BlockSpec((None, 4, Q_BLOCK_A, 2 * Q_BLOCK_A), lambda b, h, r: (h, 0, 0, 0))],
        out_specs=[out_spec, out_spec],
        out_shape=[out_shape, out_shape],
        scratch_shapes=[
            pltpu.VMEM((n_res, sub_len + Q_BLOCK_A, HEAD_DIM_A), BF16),
            pltpu.VMEM((n_res, sub_len + Q_BLOCK_A, HEAD_DIM_A), BF16),
        ],
        compiler_params=_params("parallel", "parallel", "arbitrary"),
        name=f"window_attn_g{group}",
    )(qkv_g, qkv_g, qkv_g, _alibi_band_bias(group, d))


def _mix_out_proj_kernel(x_ref, o0_ref, o1_ref, o2_ref, l0_ref, l1_ref, l2_ref, w_ref, y_ref):
    cols = []
    for h in range(HEADS_A):
        l0, l1, l2 = l0_ref[h], l1_ref[h], l2_ref[h]
        mx = jnp.maximum(jnp.maximum(l0, l1), l2)
        e0, e1, e2 = jnp.exp(l0 - mx), jnp.exp(l1 - mx), jnp.exp(l2 - mx)
        mixed = (e0 * o0_ref[h] + e1 * o1_ref[h] + e2 * o2_ref[h]) / (e0 + e1 + e2)
        cols.append(mixed.astype(BF16))
    mixed = jnp.concatenate(cols, axis=-1)
    y_ref[...] = x_ref[...] + jnp.dot(mixed, w_ref[...], preferred_element_type=F32)


def _mix_out_proj(x, outs, lses, w_o, layer, *, tm=256):
    m, d = x.shape
    row = lambda i: (i, 0)
    return pl.pallas_call(
        _mix_out_proj_kernel,
        grid=(m // tm,),
        in_specs=[pl.BlockSpec((tm, d), row)]
        + [pl.BlockSpec((HEADS_A, tm, HEAD_DIM_A), lambda i: (0, i, 0))] * 6
        + [pl.BlockSpec((None, GROUP_WIDTH_A, d), lambda i: (layer, 0, 0))],
        out_specs=pl.BlockSpec((tm, d), row),
        out_shape=jax.ShapeDtypeStruct((m, d), F32),
        compiler_params=_params("parallel"),
        name="mix_out_proj",
    )(x, *outs, *lses, w_o)


def _latent_q_kernel(c_ref, g_ref, w_ref, cos_ref, sin_ref, q_ref):
    h = _rmsnorm(c_ref[...], g_ref[...]).astype(BF16)
    full = jnp.dot(h, w_ref[...], preferred_element_type=F32)
    cos_t, sin_t = cos_ref[...], sin_ref[...]
    per = QK_NOPE_DIM + 2 * LANES
    for hd in range(N_HEADS_B):
        base = hd * per
        nope = full[:, base:base + QK_NOPE_DIM]
        rope = full[:, base + QK_NOPE_DIM:base + QK_NOPE_DIM + LANES]
        swapped = full[:, base + QK_NOPE_DIM + LANES:base + per]
        q_ref[hd, :, 0:QK_NOPE_DIM] = nope.astype(BF16)
        q_ref[hd, :, QK_NOPE_DIM:QK_PAD_B] = (rope * cos_t + swapped * sin_t).astype(BF16)


def _latent_kv_kernel(c_ref, kr_ref, krs_ref, g_ref, w_ref, cos_ref, sin_ref, k_ref, v_ref):
    h = _rmsnorm(c_ref[...], g_ref[...]).astype(BF16)
    kv = jnp.dot(h, w_ref[...], preferred_element_type=F32)
    k_rope = (kr_ref[...] * cos_ref[...] + krs_ref[...] * sin_ref[...]).astype(BF16)
    per = QK_NOPE_DIM + V_HEAD_DIM
    for hd in range(N_HEADS_B):
        k_ref[hd, :, 0:QK_NOPE_DIM] = kv[:, hd * per:hd * per + QK_NOPE_DIM].astype(BF16)
        k_ref[hd, :, QK_NOPE_DIM:QK_PAD_B] = k_rope
        v_ref[hd] = kv[:, hd * per + QK_NOPE_DIM:(hd + 1) * per].astype(BF16)


def _latent_attn_kernel(q_ref, k_ref, v_ref, o_ref, *, n_parts):
    c = (QK_NOPE_DIM + QK_ROPE_DIM) ** -0.5 * math.log2(math.e)
    k = k_ref[...]
    v = v_ref[...]
    rows = q_ref.shape[0] // n_parts
    for part in range(n_parts):
        sl = slice(part * rows, (part + 1) * rows)
        s = lax.dot_general(q_ref[sl, :], k, (((1,), (1,)), ((), ())), preferred_element_type=F32)
        m = jnp.max(s, axis=-1, keepdims=True)
        p = jnp.exp2((s - m) * c)
        l = jnp.sum(p, axis=-1, keepdims=True)
        o = jnp.dot(p.astype(BF16), v, preferred_element_type=F32)
        o_ref[sl, :] = (o / l).astype(o_ref.dtype)


def _rope_tables(seq_len):
    pos = jnp.arange(seq_len, dtype=F32)
    inv_freq = ROPE_THETA ** (-jnp.arange(0, QK_ROPE_DIM, 2, dtype=F32) / QK_ROPE_DIM)
    ang = pos[:, None] * inv_freq[None, :]
    cos, sin = jnp.cos(ang), jnp.sin(ang)
    zeros = jnp.zeros((seq_len, LANES - QK_ROPE_DIM), F32)
    return jnp.concatenate([cos, cos, zeros], axis=-1), jnp.concatenate([-sin, sin, zeros], axis=-1)


def _latent_weights(w_in, w_uq):
    half = QK_ROPE_DIM // 2
    d = w_in.shape[0]
    c_q = w_in[:, :Q_LORA_RANK]
    c_kv = w_in[:, Q_LORA_RANK:Q_LORA_RANK + KV_LORA_RANK]
    x1 = w_in[:, Q_LORA_RANK + KV_LORA_RANK:Q_LORA_RANK + KV_LORA_RANK + half]
    x2 = w_in[:, Q_LORA_RANK + KV_LORA_RANK + half:]
    zpad = jnp.zeros((d, LANES - QK_ROPE_DIM), w_in.dtype)
    w_in_p = jnp.concatenate([c_kv, x1, x2, zpad, x2, x1, zpad, c_q], axis=-1)

    w = w_uq.reshape(Q_LORA_RANK, N_HEADS_B, QK_NOPE_DIM + QK_ROPE_DIM)
    nope = w[:, :, :QK_NOPE_DIM]
    q1 = w[:, :, QK_NOPE_DIM:QK_NOPE_DIM + half]
    q2 = w[:, :, QK_NOPE_DIM + half:]
    zq = jnp.zeros((Q_LORA_RANK, N_HEADS_B, LANES - QK_ROPE_DIM), w_uq.dtype)
    w_uq_p = jnp.concatenate([nope, q1, q2, zq, q2, q1, zq], axis=-1).reshape(Q_LORA_RANK, -1)
    return w_in_p, w_uq_p


def _latent_qkv(c, q_norm, kv_norm, w_uq_p, w_ukv, cos_t, sin_t, layer, seq_len, *, tm=256):
    m = c.shape[0]
    n_pos_blocks = seq_len // tm
    row = lambda i: (i, 0)
    pos = lambda i: (i % n_pos_blocks, 0)
    const = lambda i: (0, 0)
    head_major = lambda i: (0, i, 0)
    q = pl.pallas_call(
        _latent_q_kernel,
        grid=(m // tm,),
        in_specs=[
            pl.BlockSpec((tm, Q_LORA_RANK), lambda i: (i, 1)),
            pl.BlockSpec((None, 1, Q_LORA_RANK), lambda i: (layer, 0, 0)),
            pl.BlockSpec(w_uq_p.shape, const),
            pl.BlockSpec((tm, LANES), pos),
            pl.BlockSpec((tm, LANES), pos),
        ],
        out_specs=pl.BlockSpec((N_HEADS_B, tm, QK_PAD_B), head_major),
        out_shape=jax.ShapeDtypeStruct((N_HEADS_B, m, QK_PAD_B), BF16),
        compiler_params=_params("parallel"),
        name="latent_q",
    )(c, q_norm.reshape(-1, 1, Q_LORA_RANK), w_uq_p, cos_t, sin_t)
    kr_block = KV_LORA_RANK // LANES
    k, v = pl.pallas_call(
        _latent_kv_kernel,
        grid=(m // tm,),
        in_specs=[
            pl.BlockSpec((tm, KV_LORA_RANK), row),
            pl.BlockSpec((tm, LANES), lambda i: (i, kr_block)),
            pl.BlockSpec((tm, LANES), lambda i: (i, kr_block + 1)),
            pl.BlockSpec((None, 1, KV_LORA_RANK), lambda i: (layer, 0, 0)),
            pl.BlockSpec((None,) + w_ukv.shape[1:], lambda i: (layer, 0, 0)),
            pl.BlockSpec((tm, LANES), pos),
            pl.BlockSpec((tm, LANES), pos),
        ],
        out_specs=[
            pl.BlockSpec((N_HEADS_B, tm, QK_PAD_B), head_major),
            pl.BlockSpec((N_HEADS_B, tm, V_HEAD_DIM), head_major),
        ],
        out_shape=[
            jax.ShapeDtypeStruct((N_HEADS_B, m, QK_PAD_B), BF16),
            jax.ShapeDtypeStruct((N_HEADS_B, m, V_HEAD_DIM), BF16),
        ],
        compiler_params=_params("parallel"),
        name="latent_kv",
    )(c, c, c, kv_norm.reshape(-1, 1, KV_LORA_RANK), w_ukv, cos_t, sin_t)
    return q, k, v


def _latent_attn(q, k, v, seq_len, *, part_rows=256):
    m = q.shape[1]
    n_seq = m // seq_len
    tq = min(seq_len, 2048)
    n_q = seq_len // tq
    return pl.pallas_call(
        functools.partial(_latent_attn_kernel, n_parts=tq // part_rows),
        grid=(n_seq, N_HEADS_B, n_q),
        in_specs=[
            pl.BlockSpec((None, tq, QK_PAD_B), lambda b, h, i: (h, b * n_q + i, 0)),
            pl.BlockSpec((None, seq_len, QK_PAD_B), lambda b, h, i: (h, b, 0)),
            pl.BlockSpec((None, seq_len, V_HEAD_DIM), lambda b, h, i: (h, b, 0)),
        ],
        out_specs=pl.BlockSpec((None, tq, V_HEAD_DIM), lambda b, h, i: (h, b * n_q + i, 0)),
        out_shape=jax.ShapeDtypeStruct((N_HEADS_B, m, V_HEAD_DIM), BF16),
        compiler_params=_params("parallel", "parallel", "arbitrary"),
        name="latent_attn",
    )(q, k, v)


def _out_proj_kernel(x_ref, a_ref, w_ref, y_ref):
    a = jnp.concatenate([a_ref[h] for h in range(a_ref.shape[0])], axis=-1)
    y_ref[...] = x_ref[...] + jnp.dot(a, w_ref[...], preferred_element_type=F32)


def _out_proj(x, a, w, layer, *, tm=256):
    m, d = x.shape
    n_h, _, hd = a.shape
    row = lambda i: (i, 0)
    return pl.pallas_call(
        _out_proj_kernel,
        grid=(m // tm,),
        in_specs=[
            pl.BlockSpec((tm, d), row),
            pl.BlockSpec((n_h, tm, hd), lambda i: (0, i, 0)),
            pl.BlockSpec((None, n_h * hd, d), lambda i: (layer, 0, 0)),
        ],
        out_specs=pl.BlockSpec((tm, d), row),
        out_shape=jax.ShapeDtypeStruct((m, d), F32),
        compiler_params=_params("parallel"),
        name="out_proj",
    )(x, a, w)


def _dilated_mixture_mixer(x, p, i, j, seq_len):
    qkv = _qkv_proj(x, p['mix_norm'], p['a_w_qkv'], i, j, seq_len)
    outs, lses = [], []
    for group in range(N_GROUPS_A):
        o, lse = _window_attn(qkv[group], group, seq_len)
        outs.append(o)
        lses.append(lse)
    return _mix_out_proj(x, outs, lses, p['a_w_o'], j)


def _latent_attention_mixer(x, p, i, j, seq_len):
    w_in_p, w_uq_p = p['b_latent'][j]
    c = _norm_proj(x, p['mix_norm'][i], w_in_p, F32, tm=512, tn=w_in_p.shape[1])
    cos_t, sin_t = _rope_tables(seq_len)
    q, k, v = _latent_qkv(c, p['b_q_norm'], p['b_kv_norm'], w_uq_p, p['b_w_ukv'], cos_t, sin_t, j, seq_len)
    o = _latent_attn(q, k, v, seq_len)
    return _out_proj(x, o, p['b_w_o'], j)


def _trunk(x, seq_len, p):
    depth = p['ffn1_norm'].shape[0]
    for i in range(depth):
        x = _ffn(x, p['ffn1_norm'], p['ffn1_w_gate'], p['ffn1_w_up'], p['ffn1_w_down'], i)
        j = i // 2
        if i % 2 == 0:
            x = _dilated_mixture_mixer(x, p, i, j, seq_len)
        else:
            x = _latent_attention_mixer(x, p, i, j, seq_len)
        final_g = p['final_norm'] if i == depth - 1 else None
        x = _ffn(x, p['ffn2_norm'], p['ffn2_w_gate'], p['ffn2_w_up'], p['ffn2_w_down'], i, final_g)
    return x


def kernel(x_prompt, x_sample, ffn1_norm, ffn1_w_gate, ffn1_w_up, ffn1_w_down, mix_norm, a_w_qkv, a_w_o, b_w_in, b_q_norm, b_w_uq, b_kv_norm, b_w_ukv, b_w_o, ffn2_norm, ffn2_w_gate, ffn2_w_up, ffn2_w_down, final_norm):
    n_b = b_w_in.shape[0]
    latent = [_latent_weights(b_w_in[j], b_w_uq[j]) for j in range(n_b)]
    p = dict(
        ffn1_norm=ffn1_norm, ffn2_norm=ffn2_norm, mix_norm=mix_norm, final_norm=final_norm,
        b_q_norm=b_q_norm, b_kv_norm=b_kv_norm,
        ffn1_w_gate=ffn1_w_gate.astype(BF16), ffn1_w_up=ffn1_w_up.astype(BF16), ffn1_w_down=ffn1_w_down.astype(BF16),
        ffn2_w_gate=ffn2_w_gate.astype(BF16), ffn2_w_up=ffn2_w_up.astype(BF16), ffn2_w_down=ffn2_w_down.astype(BF16),
        a_w_qkv=a_w_qkv.astype(BF16), a_w_o=a_w_o.astype(BF16),
        b_latent=[(w_in_p.astype(BF16), w_uq_p.astype(BF16)) for w_in_p, w_uq_p in latent],
        b_w_ukv=b_w_ukv.astype(BF16), b_w_o=b_w_o.astype(BF16),
    )
    outs = []
    for x in (x_prompt, x_sample):
        b, s, d = x.shape
        outs.append(_trunk(x.reshape(b * s, d), s, p).reshape(b, s, d))
    return tuple(outs)
```

```python
import functools
import math

import numpy as np
import jax
import jax.numpy as jnp
from jax import lax
from jax.experimental import pallas as pl
from jax.experimental.pallas import tpu as pltpu

F32 = jnp.float32
BF16 = jnp.bfloat16

NORM_EPS = 1e-6
NEG_INF = -1e30
NORM_CHUNK_ROWS = 256
LANES = 128

DIL_GROUPS = ((128, 1), (512, 4), (2048, 16))
N_GROUPS_A = 3
HEADS_A = 8
HEAD_DIM_A = 128
GROUP_WIDTH_A = HEADS_A * HEAD_DIM_A
Q_BLOCK_A = 128
PERM_ROWS_A = 512
CHAINS_A = 8
EDGE_FIRST, EDGE_LAST = 1, 2
N_HEADS_B = 16
Q_LORA_RANK = 768
KV_LORA_RANK = 512
QK_NOPE_DIM = 128
QK_ROPE_DIM = 64
V_HEAD_DIM = 128
ROPE_THETA = 10000.0
QK_PAD_B = 256

VMEM_LIMIT_BYTES = 48 * 1024 * 1024
VMEM_LIMIT_LARGE_BYTES = 58 * 1024 * 1024


def _params(*sem, vmem_limit_bytes=VMEM_LIMIT_BYTES):
    return pltpu.CompilerParams(dimension_semantics=sem, vmem_limit_bytes=vmem_limit_bytes)


def _rmsnorm(x, g):
    ms = jnp.mean(x * x, axis=-1, keepdims=True)
    return x * lax.rsqrt(ms + NORM_EPS) * g


def _ffn_kernel(*refs, final):
    if final:
        x_ref, g_ref, wg_ref, wu_ref, wd_ref, fg_ref, o_ref, h_ref = refs
    else:
        x_ref, g_ref, wg_ref, wu_ref, wd_ref, o_ref, h_ref = refs
    j = pl.program_id(1)
    last = pl.num_programs(1) - 1
    tm = x_ref.shape[0]
    chunks = [slice(r, r + NORM_CHUNK_ROWS) for r in range(0, tm, NORM_CHUNK_ROWS)]

    def partial_down(h):
        gate = jnp.dot(h, wg_ref[...], preferred_element_type=F32)
        up = jnp.dot(h, wu_ref[...], preferred_element_type=F32)
        act = (gate * jax.nn.sigmoid(gate) * up).astype(BF16)
        return jnp.dot(act, wd_ref[...], preferred_element_type=F32)

    @pl.when(j == 0)
    def _():
        for rows in chunks:
            h = _rmsnorm(x_ref[rows, :], g_ref[...]).astype(BF16)
            h_ref[rows, :] = h
            o_ref[rows, :] = partial_down(h)

    @pl.when((j > 0) & (j < last))
    def _():
        o_ref[...] += partial_down(h_ref[...])

    @pl.when(j == last)
    def _():
        for rows in chunks:
            y = x_ref[rows, :] + 0.5 * (o_ref[rows, :] + partial_down(h_ref[rows, :]))
            if final:
                y = _rmsnorm(y, fg_ref[...])
            o_ref[rows, :] = y


def _ffn(x, g, wg, wu, wd, layer, final_g=None, *, tm=1024, tf=512):
    m, d = x.shape
    f = wg.shape[2]
    assert f // tf >= 2, "the kernel's first and last ff steps must be distinct"
    final = final_g is not None
    in_specs = [
        pl.BlockSpec((tm, d), lambda i, j: (i, 0)),
        pl.BlockSpec((None, 1, d), lambda i, j: (layer, 0, 0)),
        pl.BlockSpec((None, d, tf), lambda i, j: (layer, 0, j)),
        pl.BlockSpec((None, d, tf), lambda i, j: (layer, 0, j)),
        pl.BlockSpec((None, tf, d), lambda i, j: (layer, j, 0)),
    ]
    args = [x, g.reshape(-1, 1, d), wg, wu, wd]
    if final:
        in_specs.append(pl.BlockSpec((1, d), lambda i, j: (0, 0)))
        args.append(final_g.reshape(1, d))
    return pl.pallas_call(
        functools.partial(_ffn_kernel, final=final),
        grid=(m // tm, f // tf),
        in_specs=in_specs,
        out_specs=pl.BlockSpec((tm, d), lambda i, j: (i, 0)),
        out_shape=jax.ShapeDtypeStruct((m, d), F32),
        scratch_shapes=[pltpu.VMEM((tm, d), BF16)],
        compiler_params=_params("parallel", "arbitrary", vmem_limit_bytes=VMEM_LIMIT_LARGE_BYTES),
        name="ffn",
    )(*args)


def _norm_proj_kernel(x_ref, g_ref, w_ref, o_ref):
    for r in range(0, x_ref.shape[0], NORM_CHUNK_ROWS):
        rows = slice(r, r + NORM_CHUNK_ROWS)
        h = _rmsnorm(x_ref[rows, :], g_ref[...]).astype(BF16)
        o_ref[rows, :] = jnp.dot(h, w_ref[...], preferred_element_type=F32).astype(o_ref.dtype)


def _norm_proj(x, g, w, out_dtype, *, tm=512):
    m, d = x.shape
    n = w.shape[1]
    return pl.pallas_call(
        _norm_proj_kernel,
        grid=(m // tm,),
        in_specs=[
            pl.BlockSpec((tm, d), lambda i: (i, 0)),
            pl.BlockSpec((1, d), lambda i: (0, 0)),
            pl.BlockSpec((d, n), lambda i: (0, 0)),
        ],
        out_specs=pl.BlockSpec((tm, n), lambda i: (i, 0)),
        out_shape=jax.ShapeDtypeStruct((m, n), out_dtype),
        compiler_params=_params("parallel"),
        name="norm_proj",
    )(x, g.reshape(1, d), w)


def _qkv_proj_kernel(x_ref, g_ref, perm_ref, w_ref, o0_ref, o1_ref, o2_ref, hnat_ref, hperm_ref):
    j = pl.program_id(1)

    @pl.when(j == 0)
    def _():
        h = _rmsnorm(x_ref[...], g_ref[...]).astype(BF16)
        hnat_ref[...] = h
        hperm_ref[...] = h

    tm = x_ref.shape[0]
    n_sub = tm // PERM_ROWS_A

    @pl.when((j == 3) | (j == 6))
    def _():
        for sub in range(n_sub):
            sl = slice(sub * PERM_ROWS_A, (sub + 1) * PERM_ROWS_A)
            hperm_ref[sl, :] = jnp.dot(perm_ref[...], hnat_ref[sl, :], preferred_element_type=F32).astype(BF16)

    res = jnp.dot(hperm_ref[...], w_ref[...], preferred_element_type=F32)
    for group, o_ref in enumerate((o0_ref, o1_ref, o2_ref)):
        dil = DIL_GROUPS[group][1]
        rows = PERM_ROWS_A // dil

        @pl.when(j // 3 == group)
        def _(o_ref=o_ref, dil=dil, rows=rows):
            for sub in range(n_sub):
                for hd in range(HEADS_A):
                    for r in range(dil):
                        src = sub * PERM_ROWS_A + r * rows
                        o_ref[hd, r, sub * rows:(sub + 1) * rows, :] = (
                            res[src:src + rows, hd * HEAD_DIM_A:(hd + 1) * HEAD_DIM_A].astype(BF16))


def _residue_permutations():
    tm = PERM_ROWS_A
    mats = []
    for _, dil in DIL_GROUPS:
        rows = tm // dil
        p = np.zeros((tm, tm), np.float32)
        dst = np.arange(tm)
        src = (dst % rows) * dil + dst // rows
        p[dst, src] = 1.0
        mats.append(p)
    return jnp.asarray(np.stack(mats), BF16)


def _qkv_proj(x, g, w_qkv, g_layer, w_layer, seq_len, *, tm=1024):
    m, d_model = x.shape
    n_seq = m // seq_len
    tiles_per_seq = seq_len // tm
    n_col = 3 * N_GROUPS_A

    def out_spec(group):
        dil = DIL_GROUPS[group][1]

        def index(i, j):
            t = jnp.clip(j - 3 * group, 0, 2)
            return (t, 0, i // tiles_per_seq, 0, i % tiles_per_seq, 0)

        return pl.BlockSpec((None, HEADS_A, None, dil, tm // dil, HEAD_DIM_A), index)

    out_shape = [
        jax.ShapeDtypeStruct((3, HEADS_A, n_seq, dil, seq_len // dil, HEAD_DIM_A), BF16) for _, dil in DIL_GROUPS
    ]
    return pl.pallas_call(
        _qkv_proj_kernel,
        grid=(m // tm, n_col),
        in_specs=[
            pl.BlockSpec((tm, d_model), lambda i, j: (i, 0)),
            pl.BlockSpec((None, 1, d_model), lambda i, j: (g_layer, 0, 0)),
            pl.BlockSpec((None, PERM_ROWS_A, PERM_ROWS_A), lambda i, j: (j // 3, 0, 0)),
            pl.BlockSpec((None, d_model, GROUP_WIDTH_A), lambda i, j: (w_layer, 0, (j % 3) * N_GROUPS_A + j // 3)),
        ],
        out_specs=[out_spec(group) for group in range(N_GROUPS_A)],
        out_shape=out_shape,
        scratch_shapes=[pltpu.VMEM((tm, d_model), BF16), pltpu.VMEM((tm, d_model), BF16)],
        compiler_params=_params("parallel", "arbitrary", vmem_limit_bytes=VMEM_LIMIT_LARGE_BYTES),
        name="qkv_proj",
    )(x, g.reshape(-1, 1, d_model), _residue_permutations(), w_qkv)


def _window_attn_kernel(q_ref, k_ref, v_ref, bias_ref, o_ref, lse_ref, kpad_ref, vpad_ref, *,
                        sub_len, dilation, n_res):
    half = Q_BLOCK_A // 2
    res_blk = pl.program_id(2)
    n_q = sub_len // Q_BLOCK_A
    scale = HEAD_DIM_A ** -0.5
    c = scale * math.log2(math.e)

    zeros = jnp.zeros((half, HEAD_DIM_A), BF16)
    for rr in range(n_res):
        kpad_ref[rr, 0:half, :] = zeros
        kpad_ref[rr, half + sub_len:2 * half + sub_len, :] = zeros
        kpad_ref[rr, half:half + sub_len, :] = k_ref[rr]
        vpad_ref[rr, 0:half, :] = zeros
        vpad_ref[rr, half + sub_len:2 * half + sub_len, :] = zeros
        vpad_ref[rr, half:half + sub_len, :] = v_ref[rr]

    def one_block(rr, q0, edge):
        q = q_ref[rr, pl.ds(q0, Q_BLOCK_A), :]
        kw = kpad_ref[rr, pl.ds(q0, 2 * Q_BLOCK_A), :]
        vw = vpad_ref[rr, pl.ds(q0, 2 * Q_BLOCK_A), :]
        t = lax.dot_general(q, kw, (((1,), (1,)), ((), ())), preferred_element_type=F32) + bias_ref[edge]
        m = jnp.max(t, axis=-1, keepdims=True)
        p = jnp.exp2((t - m) * c)
        l = jnp.sum(p, axis=-1, keepdims=True)
        o = jnp.dot(p.astype(BF16), vw, preferred_element_type=F32) / l
        lse = m * scale + jnp.log(l)
        if dilation == 1:
            rows = pl.ds(q0, Q_BLOCK_A)
        else:
            rows = pl.ds(q0 * dilation + res_blk * n_res + rr, Q_BLOCK_A, stride=dilation)
        o_ref[rows, :] = o
        lse_ref[rows, :] = jnp.broadcast_to(lse, (Q_BLOCK_A, LANES))

    if n_q >= CHAINS_A:
        n_it = n_q // CHAINS_A

        def body(it, carry):
            for u in range(CHAINS_A):
                if u == 0:
                    edge = jnp.where(it == 0, EDGE_FIRST, 0)
                elif u == CHAINS_A - 1:
                    edge = jnp.where(it == n_it - 1, EDGE_LAST, 0)
                else:
                    edge = 0
                one_block(0, pl.multiple_of((it * CHAINS_A + u) * Q_BLOCK_A, Q_BLOCK_A), edge)
            return carry

        lax.fori_loop(0, n_it, body, 0)
    else:
        for rr in range(n_res):
            for i in range(n_q):
                one_block(rr, i * Q_BLOCK_A, (EDGE_FIRST if i == 0 else 0) | (EDGE_LAST if i == n_q - 1 else 0))


def _alibi_band_bias(group, dilation):
    n = N_GROUPS_A * HEADS_A
    head = jnp.arange(1, n + 1, dtype=F32)
    slopes = jnp.exp2(-8.0 * head / n).reshape(N_GROUPS_A, HEADS_A)[group]
    half = Q_BLOCK_A // 2
    r = np.arange(Q_BLOCK_A)[:, None]
    c = np.arange(2 * Q_BLOCK_A)[None, :]
    rel = np.abs(c - r - half)
    valid = []
    for edge in range(4):
        v = rel <= half
        if edge & EDGE_FIRST:
            v = v & (c >= half)
        if edge & EDGE_LAST:
            v = v & (c < Q_BLOCK_A + half)
        valid.append(v)
    valid = jnp.asarray(np.stack(valid))
    bias = -slopes[:, None, None, None] * jnp.asarray(rel * dilation, F32)[None, None]
    return jnp.where(valid[None], bias, NEG_INF) * (HEAD_DIM_A ** 0.5)


def _window_attn(qkv_g, group, seq_len):
    _, _, n_seq, d, sub_len, _ = qkv_g.shape
    m = n_seq * seq_len
    n_q = sub_len // Q_BLOCK_A
    n_res = 1 if n_q >= CHAINS_A else min(d, CHAINS_A // n_q)

    def in_spec(t):
        return pl.BlockSpec((None, None, None, n_res, sub_len, HEAD_DIM_A), lambda b, h, r: (t, h, b, r, 0, 0))

    out_spec = pl.BlockSpec((None, seq_len, HEAD_DIM_A), lambda b, h, r: (h, b, 0))
    out_shape = jax.ShapeDtypeStruct((HEADS_A, m, HEAD_DIM_A), F32)
    return pl.pallas_call(
        functools.partial(_window_attn_kernel, sub_len=sub_len, dilation=d, n_res=n_res),
        grid=(n_seq, HEADS_A, d // n_res),
        in_specs=[in_spec(0), in_spec(1), in_spec(2),
                  pl.BlockSpec((None, 4, Q_BLOCK_A, 2 * Q_BLOCK_A), lambda b, h, r: (h, 0, 0, 0))],
        out_specs=[out_spec, out_spec],
        out_shape=[out_shape, out_shape],
        scratch_shapes=[
            pltpu.VMEM((n_res, sub_len + Q_BLOCK_A, HEAD_DIM_A), BF16),
            pltpu.VMEM((n_res, sub_len + Q_BLOCK_A, HEAD_DIM_A), BF16),
        ],
        compiler_params=_params("parallel", "parallel", "arbitrary"),
        name=f"window_attn_g{group}",
    )(qkv_g, qkv_g, qkv_g, _alibi_band_bias(group, d))


def _mix_out_proj_kernel(x_ref, o0_ref, o1_ref, o2_ref, l0_ref, l1_ref, l2_ref, w_ref, y_ref):
    cols = []
    for h in range(HEADS_A):
        l0, l1, l2 = l0_ref[h], l1_ref[h], l2_ref[h]
        mx = jnp.maximum(jnp.maximum(l0, l1), l2)
        e0, e1, e2 = jnp.exp(l0 - mx), jnp.exp(l1 - mx), jnp.exp(l2 - mx)
        mixed = (e0 * o0_ref[h] + e1 * o1_ref[h] + e2 * o2_ref[h]) / (e0 + e1 + e2)
        cols.append(mixed.astype(BF16))
    mixed = jnp.concatenate(cols, axis=-1)
    y_ref[...] = x_ref[...] + jnp.dot(mixed, w_ref[...], preferred_element_type=F32)


def _mix_out_proj(x, outs, lses, w_o, layer, *, tm=256):
    m, d = x.shape
    row = lambda i: (i, 0)
    return pl.pallas_call(
        _mix_out_proj_kernel,
        grid=(m // tm,),
        in_specs=[pl.BlockSpec((tm, d), row)]
        + [pl.BlockSpec((HEADS_A, tm, HEAD_DIM_A), lambda i: (0, i, 0))] * 6
        + [pl.BlockSpec((None, GROUP_WIDTH_A, d), lambda i: (layer, 0, 0))],
        out_specs=pl.BlockSpec((tm, d), row),
        out_shape=jax.ShapeDtypeStruct((m, d), F32),
        compiler_params=_params("parallel"),
        name="mix_out_proj",
    )(x, *outs, *lses, w_o)


def _latent_q_kernel(c_ref, g_ref, w_ref, cos_ref, sin_ref, q_ref):
    h = _rmsnorm(c_ref[...], g_ref[...]).astype(BF16)
    full = jnp.dot(h, w_ref[...], preferred_element_type=F32)
    cos_t, sin_t = cos_ref[...], sin_ref[...]
    per = QK_NOPE_DIM + 2 * LANES
    for hd in range(N_HEADS_B):
        base = hd * per
        nope = full[:, base:base + QK_NOPE_DIM]
        rope = full[:, base + QK_NOPE_DIM:base + QK_NOPE_DIM + LANES]
        swapped = full[:, base + QK_NOPE_DIM + LANES:base + per]
        q_ref[hd, :, 0:QK_NOPE_DIM] = nope.astype(BF16)
        q_ref[hd, :, QK_NOPE_DIM:QK_PAD_B] = (rope * cos_t + swapped * sin_t).astype(BF16)


def _latent_kv_kernel(c_ref, kr_ref, krs_ref, g_ref, w_ref, cos_ref, sin_ref, k_ref, v_ref):
    h = _rmsnorm(c_ref[...], g_ref[...]).astype(BF16)
    kv = jnp.dot(h, w_ref[...], preferred_element_type=F32)
    k_rope = (kr_ref[...] * cos_ref[...] + krs_ref[...] * sin_ref[...]).astype(BF16)
    per = QK_NOPE_DIM + V_HEAD_DIM
    for hd in range(N_HEADS_B):
        k_ref[hd, :, 0:QK_NOPE_DIM] = kv[:, hd * per:hd * per + QK_NOPE_DIM].astype(BF16)
        k_ref[hd, :, QK_NOPE_DIM:QK_PAD_B] = k_rope
        v_ref[hd] = kv[:, hd * per + QK_NOPE_DIM:(hd + 1) * per].astype(BF16)


def _latent_attn_kernel(q_ref, k_ref, v_ref, o_ref, *, n_parts):
    c = (QK_NOPE_DIM + QK_ROPE_DIM) ** -0.5 * math.log2(math.e)
    k = k_ref[...]
    v = v_ref[...]
    rows = q_ref.shape[0] // n_parts
    for part in range(n_parts):
        sl = slice(part * rows, (part + 1) * rows)
        s = lax.dot_general(q_ref[sl, :], k, (((1,), (1,)), ((), ())), preferred_element_type=F32)
        m = jnp.max(s, axis=-1, keepdims=True)
        p = jnp.exp2((s - m) * c)
        l = jnp.sum(p, axis=-1, keepdims=True)
        o = jnp.dot(p.astype(BF16), v, preferred_element_type=F32)
        o_ref[sl, :] = (o / l).astype(o_ref.dtype)


def _rope_tables(seq_len):
    pos = jnp.arange(seq_len, dtype=F32)
    inv_freq = ROPE_THETA ** (-jnp.arange(0, QK_ROPE_DIM, 2, dtype=F32) / QK_ROPE_DIM)
    ang = pos[:, None] * inv_freq[None, :]
    cos, sin = jnp.cos(ang), jnp.sin(ang)
    zeros = jnp.zeros((seq_len, LANES - QK_ROPE_DIM), F32)
    return jnp.concatenate([cos, cos, zeros], axis=-1), jnp.concatenate([-sin, sin, zeros], axis=-1)


def _latent_weights(w_in, w_uq):
    half = QK_ROPE_DIM // 2
    d = w_in.shape[0]
    c_q = w_in[:, :Q_LORA_RANK]
    c_kv = w_in[:, Q_LORA_RANK:Q_LORA_RANK + KV_LORA_RANK]
    x1 = w_in[:, Q_LORA_RANK + KV_LORA_RANK:Q_LORA_RANK + KV_LORA_RANK + half]
    x2 = w_in[:, Q_LORA_RANK + KV_LORA_RANK + half:]
    zpad = jnp.zeros((d, LANES - QK_ROPE_DIM), w_in.dtype)
    w_in_p = jnp.concatenate([c_kv, x1, x2, zpad, x2, x1, zpad, c_q], axis=-1)

    w = w_uq.reshape(Q_LORA_RANK, N_HEADS_B, QK_NOPE_DIM + QK_ROPE_DIM)
    nope = w[:, :, :QK_NOPE_DIM]
    q1 = w[:, :, QK_NOPE_DIM:QK_NOPE_DIM + half]
    q2 = w[:, :, QK_NOPE_DIM + half:]
    zq = jnp.zeros((Q_LORA_RANK, N_HEADS_B, LANES - QK_ROPE_DIM), w_uq.dtype)
    w_uq_p = jnp.concatenate([nope, q1, q2, zq, q2, q1, zq], axis=-1).reshape(Q_LORA_RANK, -1)
    return w_in_p, w_uq_p


def _latent_qkv(c, q_norm, kv_norm, w_uq_p, w_ukv, cos_t, sin_t, layer, seq_len, *, tm=256):
    m = c.shape[0]
    n_pos_blocks = seq_len // tm
    row = lambda i: (i, 0)
    pos = lambda i: (i % n_pos_blocks, 0)
    const = lambda i: (0, 0)
    head_major = lambda i: (0, i, 0)
    q = pl.pallas_call(
        _latent_q_kernel,
        grid=(m // tm,),
        in_specs=[
            pl.BlockSpec((tm, Q_LORA_RANK), lambda i: (i, 1)),
            pl.BlockSpec((None, 1, Q_LORA_RANK), lambda i: (layer, 0, 0)),
            pl.BlockSpec(w_uq_p.shape, const),
            pl.BlockSpec((tm, LANES), pos),
            pl.BlockSpec((tm, LANES), pos),
        ],
        out_specs=pl.BlockSpec((N_HEADS_B, tm, QK_PAD_B), head_major),
        out_shape=jax.ShapeDtypeStruct((N_HEADS_B, m, QK_PAD_B), BF16),
        compiler_params=_params("parallel"),
        name="latent_q",
    )(c, q_norm.reshape(-1, 1, Q_LORA_RANK), w_uq_p, cos_t, sin_t)
    kr_block = KV_LORA_RANK // LANES
    k, v = pl.pallas_call(
        _latent_kv_kernel,
        grid=(m // tm,),
        in_specs=[
            pl.BlockSpec((tm, KV_LORA_RANK), row),
            pl.BlockSpec((tm, LANES), lambda i: (i, kr_block)),
            pl.BlockSpec((tm, LANES), lambda i: (i, kr_block + 1)),
            pl.BlockSpec((None, 1, KV_LORA_RANK), lambda i: (layer, 0, 0)),
            pl.BlockSpec((None,) + w_ukv.shape[1:], lambda i: (layer, 0, 0)),
            pl.BlockSpec((tm, LANES), pos),
            pl.BlockSpec((tm, LANES), pos),
        ],
        out_specs=[
            pl.BlockSpec((N_HEADS_B, tm, QK_PAD_B), head_major),
            pl.BlockSpec((N_HEADS_B, tm, V_HEAD_DIM), head_major),
        ],
        out_shape=[
            jax.ShapeDtypeStruct((N_HEADS_B, m, QK_PAD_B), BF16),
            jax.ShapeDtypeStruct((N_HEADS_B, m, V_HEAD_DIM), BF16),
        ],
        compiler_params=_params("parallel"),
        name="latent_kv",
    )(c, c, c, kv_norm.reshape(-1, 1, KV_LORA_RANK), w_ukv, cos_t, sin_t)
    return q, k, v


def _latent_attn(q, k, v, seq_len, *, part_rows=256):
    m = q.shape[1]
    n_seq = m // seq_len
    tq = min(seq_len, 2048)
    n_q = seq_len // tq
    return pl.pallas_call(
        functools.partial(_latent_attn_kernel, n_parts=tq // part_rows),
        grid=(n_seq, N_HEADS_B, n_q),
        in_specs=[
            pl.BlockSpec((None, tq, QK_PAD_B), lambda b, h, i: (h, b * n_q + i, 0)),
            pl.BlockSpec((None, seq_len, QK_PAD_B), lambda b, h, i: (h, b, 0)),
            pl.BlockSpec((None, seq_len, V_HEAD_DIM), lambda b, h, i: (h, b, 0)),
        ],
        out_specs=pl.BlockSpec((None, tq, V_HEAD_DIM), lambda b, h, i: (h, b * n_q + i, 0)),
        out_shape=jax.ShapeDtypeStruct((N_HEADS_B, m, V_HEAD_DIM), BF16),
        compiler_params=_params("parallel", "parallel", "arbitrary"),
        name="latent_attn",
    )(q, k, v)


def _out_proj_kernel(x_ref, a_ref, w_ref, y_ref):
    a = jnp.concatenate([a_ref[h] for h in range(a_ref.shape[0])], axis=-1)
    y_ref[...] = x_ref[...] + jnp.dot(a, w_ref[...], preferred_element_type=F32)


def _out_proj(x, a, w, layer, *, tm=256):
    m, d = x.shape
    n_h, _, hd = a.shape
    row = lambda i: (i, 0)
    return pl.pallas_call(
        _out_proj_kernel,
        grid=(m // tm,),
        in_specs=[
            pl.BlockSpec((tm, d), row),
            pl.BlockSpec((n_h, tm, hd), lambda i: (0, i, 0)),
            pl.BlockSpec((None, n_h * hd, d), lambda i: (layer, 0, 0)),
        ],
        out_specs=pl.BlockSpec((tm, d), row),
        out_shape=jax.ShapeDtypeStruct((m, d), F32),
        compiler_params=_params("parallel"),
        name="out_proj",
    )(x, a, w)


def _dilated_mixture_mixer(x, p, i, j, seq_len):
    qkv = _qkv_proj(x, p['mix_norm'], p['a_w_qkv'], i, j, seq_len)
    outs, lses = [], []
    for group in range(N_GROUPS_A):
        o, lse = _window_attn(qkv[group], group, seq_len)
        outs.append(o)
        lses.append(lse)
    return _mix_out_proj(x, outs, lses, p['a_w_o'], j)


def _latent_attention_mixer(x, p, i, j, seq_len):
    w_in_p, w_uq_p = p['b_latent'][j]
    c = _norm_proj(x, p['mix_norm'][i], w_in_p, F32)
    cos_t, sin_t = _rope_tables(seq_len)
    q, k, v = _latent_qkv(c, p['b_q_norm'], p['b_kv_norm'], w_uq_p, p['b_w_ukv'], cos_t, sin_t, j, seq_len)
    o = _latent_attn(q, k, v, seq_len)
    return _out_proj(x, o, p['b_w_o'], j)


def _trunk(x, seq_len, p):
    depth = p['ffn1_norm'].shape[0]
    for i in range(depth):
        x = _ffn(x, p['ffn1_norm'], p['ffn1_w_gate'], p['ffn1_w_up'], p['ffn1_w_down'], i)
        j = i // 2
        if i % 2 == 0:
            x = _dilated_mixture_mixer(x, p, i, j, seq_len)
        else:
            x = _latent_attention_mixer(x, p, i, j, seq_len)
        final_g = p['final_norm'] if i == depth - 1 else None
        x = _ffn(x, p['ffn2_norm'], p['ffn2_w_gate'], p['ffn2_w_up'], p['ffn2_w_down'], i, final_g)
    return x


def kernel(x_prompt, x_sample, ffn1_norm, ffn1_w_gate, ffn1_w_up, ffn1_w_down, mix_norm, a_w_qkv, a_w_o, b_w_in, b_q_norm, b_w_uq, b_kv_norm, b_w_ukv, b_w_o, ffn2_norm, ffn2_w_gate, ffn2_w_up, ffn2_w_down, final_norm):
    n_b = b_w_in.shape[0]
    latent = [_latent_weights(b_w_in[j], b_w_uq[j]) for j in range(n_b)]
    p = dict(
        ffn1_norm=ffn1_norm, ffn2_norm=ffn2_norm, mix_norm=mix_norm, final_norm=final_norm,
        b_q_norm=b_q_norm, b_kv_norm=b_kv_norm,
        ffn1_w_gate=ffn1_w_gate.astype(BF16), ffn1_w_up=ffn1_w_up.astype(BF16), ffn1_w_down=ffn1_w_down.astype(BF16),
        ffn2_w_gate=ffn2_w_gate.astype(BF16), ffn2_w_up=ffn2_w_up.astype(BF16), ffn2_w_down=ffn2_w_down.astype(BF16),
        a_w_qkv=a_w_qkv.astype(BF16), a_w_o=a_w_o.astype(BF16),
        b_latent=[(w_in_p.astype(BF16), w_uq_p.astype(BF16)) for w_in_p, w_uq_p in latent],
        b_w_ukv=b_w_ukv.astype(BF16), b_w_o=b_w_o.astype(BF16),
    )
    outs = []
    for x in (x_prompt, x_sample):
        b, s, d = x.shape
        outs.append(_trunk(x.reshape(b * s, d), s, p).reshape(b, s, d))
    return tuple(outs)
```

```python
import functools
import math

import numpy as np
import jax
import jax.numpy as jnp
from jax import lax
from jax.experimental import pallas as pl
from jax.experimental.pallas import tpu as pltpu

F32 = jnp.float32
BF16 = jnp.bfloat16

NORM_EPS = 1e-6
NEG_INF = -1e30
NORM_CHUNK_ROWS = 256
LANES = 128

DIL_GROUPS = ((128, 1), (512, 4), (2048, 16))
N_GROUPS_A = 3
HEADS_A = 8
HEAD_DIM_A = 128
GROUP_WIDTH_A = HEADS_A * HEAD_DIM_A
Q_BLOCK_A = 128
PERM_ROWS_A = 512
CHAINS_A = 8
EDGE_FIRST, EDGE_LAST = 1, 2
N_HEADS_B = 16
Q_LORA_RANK = 768
KV_LORA_RANK = 512
QK_NOPE_DIM = 128
QK_ROPE_DIM = 64
V_HEAD_DIM = 128
ROPE_THETA = 10000.0
QK_PAD_B = 256

VMEM_LIMIT_BYTES = 48 * 1024 * 1024
VMEM_LIMIT_LARGE_BYTES = 60 * 1024 * 1024


def _params(*sem, vmem_limit_bytes=VMEM_LIMIT_BYTES):
    return pltpu.CompilerParams(dimension_semantics=sem, vmem_limit_bytes=vmem_limit_bytes)


def _rmsnorm(x, g):
    ms = jnp.mean(x * x, axis=-1, keepdims=True)
    return x * lax.rsqrt(ms + NORM_EPS) * g


def _ffn_kernel(*refs, final, cast_next):
    refs = list(refs)
    x_ref, g_ref, wg_ref, wu_ref, wd_ref = refs[:5]
    del refs[:5]
    fg_ref = refs.pop(0) if final else None
    next_f32 = [refs.pop(0) for _ in range(3)] if cast_next else []
    o_ref = refs.pop(0)
    next_bf16 = [refs.pop(0) for _ in range(3)] if cast_next else []
    (h_ref,) = refs
    j = pl.program_id(1)
    last = pl.num_programs(1) - 1
    tm = x_ref.shape[0]
    chunks = [slice(r, r + NORM_CHUNK_ROWS) for r in range(0, tm, NORM_CHUNK_ROWS)]

    def partial_down(h):
        gate = jnp.dot(h, wg_ref[...], preferred_element_type=F32)
        up = jnp.dot(h, wu_ref[...], preferred_element_type=F32)
        act = (gate * jax.nn.sigmoid(gate) * up).astype(BF16)
        return jnp.dot(act, wd_ref[...], preferred_element_type=F32)

    def cast_next_tiles():
        for src, dst in zip(next_f32, next_bf16):
            dst[...] = src[...].astype(BF16)

    @pl.when(j == 0)
    def _():
        cast_next_tiles()
        for rows in chunks:
            h = _rmsnorm(x_ref[rows, :], g_ref[...]).astype(BF16)
            h_ref[rows, :] = h
            o_ref[rows, :] = partial_down(h)

    @pl.when((j > 0) & (j < last))
    def _():
        cast_next_tiles()
        o_ref[...] += partial_down(h_ref[...])

    @pl.when(j == last)
    def _():
        cast_next_tiles()
        for rows in chunks:
            y = x_ref[rows, :] + 0.5 * (o_ref[rows, :] + partial_down(h_ref[rows, :]))
            if final:
                y = _rmsnorm(y, fg_ref[...])
            o_ref[rows, :] = y


def _ffn(x, g, weights, final_g=None, cast_next=None, *, tm=1024, tf=512):
    m, d = x.shape
    wg, wu, wd = weights
    f = wg.shape[1]
    n_i, n_j = m // tm, f // tf
    assert n_j >= 2, "the kernel's first and last ff steps must be distinct"
    final = final_g is not None
    in_specs = [
        pl.BlockSpec((tm, d), lambda i, j: (i, 0)),
        pl.BlockSpec((1, d), lambda i, j: (0, 0)),
        pl.BlockSpec((d, tf), lambda i, j: (0, j)),
        pl.BlockSpec((d, tf), lambda i, j: (0, j)),
        pl.BlockSpec((tf, d), lambda i, j: (j, 0)),
    ]
    args = [x, g.reshape(1, d), wg, wu, wd]
    out_specs = [pl.BlockSpec((tm, d), lambda i, j: (i, 0))]
    out_shape = [jax.ShapeDtypeStruct((m, d), F32)]
    if final:
        in_specs.append(pl.BlockSpec((1, d), lambda i, j: (0, 0)))
        args.append(final_g.reshape(1, d))
    if cast_next is not None:
        *stacks, layer = cast_next
        td = d // n_i
        in_specs += [
            pl.BlockSpec((None, td, tf), lambda i, j: (layer, i, j)),
            pl.BlockSpec((None, td, tf), lambda i, j: (layer, i, j)),
            pl.BlockSpec((None, tf, td), lambda i, j: (layer, j, i)),
        ]
        args += stacks
        out_specs += [
            pl.BlockSpec((td, tf), lambda i, j: (i, j)),
            pl.BlockSpec((td, tf), lambda i, j: (i, j)),
            pl.BlockSpec((tf, td), lambda i, j: (j, i)),
        ]
        out_shape += [jax.ShapeDtypeStruct(w.shape[1:], BF16) for w in stacks]
    res = pl.pallas_call(
        functools.partial(_ffn_kernel, final=final, cast_next=cast_next is not None),
        grid=(n_i, n_j),
        in_specs=in_specs,
        out_specs=out_specs,
        out_shape=out_shape,
        scratch_shapes=[pltpu.VMEM((tm, d), BF16)],
        compiler_params=_params("parallel", "arbitrary", vmem_limit_bytes=VMEM_LIMIT_LARGE_BYTES),
        name="ffn",
    )(*args)
    return (res[0], tuple(res[1:])) if cast_next is not None else res[0]


def _norm_proj_kernel(x_ref, g_ref, w_ref, o_ref):
    for r in range(0, x_ref.shape[0], NORM_CHUNK_ROWS):
        rows = slice(r, r + NORM_CHUNK_ROWS)
        h = _rmsnorm(x_ref[rows, :], g_ref[...]).astype(BF16)
        o_ref[rows, :] = jnp.dot(h, w_ref[...], preferred_element_type=F32).astype(o_ref.dtype)


def _norm_proj(x, g, w, out_dtype, *, tm=512):
    m, d = x.shape
    n = w.shape[1]
    return pl.pallas_call(
        _norm_proj_kernel,
        grid=(m // tm,),
        in_specs=[
            pl.BlockSpec((tm, d), lambda i: (i, 0)),
            pl.BlockSpec((1, d), lambda i: (0, 0)),
            pl.BlockSpec((d, n), lambda i: (0, 0)),
        ],
        out_specs=pl.BlockSpec((tm, n), lambda i: (i, 0)),
        out_shape=jax.ShapeDtypeStruct((m, n), out_dtype),
        compiler_params=_params("parallel"),
        name="norm_proj",
    )(x, g.reshape(1, d), w)


def _qkv_proj_kernel(x_ref, g_ref, perm_ref, w_ref, o0_ref, o1_ref, o2_ref, hnat_ref, hperm_ref):
    j = pl.program_id(1)

    @pl.when(j == 0)
    def _():
        h = _rmsnorm(x_ref[...], g_ref[...]).astype(BF16)
        hnat_ref[...] = h
        hperm_ref[...] = h

    tm = x_ref.shape[0]
    n_sub = tm // PERM_ROWS_A

    @pl.when((j == 3) | (j == 6))
    def _():
        for sub in range(n_sub):
            sl = slice(sub * PERM_ROWS_A, (sub + 1) * PERM_ROWS_A)
            hperm_ref[sl, :] = jnp.dot(perm_ref[...], hnat_ref[sl, :], preferred_element_type=F32).astype(BF16)

    res = jnp.dot(hperm_ref[...], w_ref[...], preferred_element_type=F32)
    for group, o_ref in enumerate((o0_ref, o1_ref, o2_ref)):
        dil = DIL_GROUPS[group][1]
        rows = PERM_ROWS_A // dil

        @pl.when(j // 3 == group)
        def _(o_ref=o_ref, dil=dil, rows=rows):
            for sub in range(n_sub):
                for hd in range(HEADS_A):
                    for r in range(dil):
                        src = sub * PERM_ROWS_A + r * rows
                        o_ref[hd, r, sub * rows:(sub + 1) * rows, :] = (
                            res[src:src + rows, hd * HEAD_DIM_A:(hd + 1) * HEAD_DIM_A].astype(BF16))


def _residue_permutations():
    tm = PERM_ROWS_A
    mats = []
    for _, dil in DIL_GROUPS:
        rows = tm // dil
        p = np.zeros((tm, tm), np.float32)
        dst = np.arange(tm)
        src = (dst % rows) * dil + dst // rows
        p[dst, src] = 1.0
        mats.append(p)
    return jnp.asarray(np.stack(mats), BF16)


def _qkv_proj(x, g, w_qkv, g_layer, w_layer, seq_len, *, tm=1024):
    m, d_model = x.shape
    n_seq = m // seq_len
    tiles_per_seq = seq_len // tm
    n_col = 3 * N_GROUPS_A

    def out_spec(group):
        dil = DIL_GROUPS[group][1]

        def index(i, j):
            t = jnp.clip(j - 3 * group, 0, 2)
            return (t, 0, i // tiles_per_seq, 0, i % tiles_per_seq, 0)

        return pl.BlockSpec((None, HEADS_A, None, dil, tm // dil, HEAD_DIM_A), index)

    out_shape = [
        jax.ShapeDtypeStruct((3, HEADS_A, n_seq, dil, seq_len // dil, HEAD_DIM_A), BF16) for _, dil in DIL_GROUPS
    ]
    return pl.pallas_call(
        _qkv_proj_kernel,
        grid=(m // tm, n_col),
        in_specs=[
            pl.BlockSpec((tm, d_model), lambda i, j: (i, 0)),
            pl.BlockSpec((None, 1, d_model), lambda i, j: (g_layer, 0, 0)),
            pl.BlockSpec((None, PERM_ROWS_A, PERM_ROWS_A), lambda i, j: (j // 3, 0, 0)),
            pl.BlockSpec((None, d_model, GROUP_WIDTH_A), lambda i, j: (w_layer, 0, (j % 3) * N_GROUPS_A + j // 3)),
        ],
        out_specs=[out_spec(group) for group in range(N_GROUPS_A)],
        out_shape=out_shape,
        scratch_shapes=[pltpu.VMEM((tm, d_model), BF16), pltpu.VMEM((tm, d_model), BF16)],
        compiler_params=_params("parallel", "arbitrary", vmem_limit_bytes=VMEM_LIMIT_LARGE_BYTES),
        name="qkv_proj",
    )(x, g.reshape(-1, 1, d_model), _residue_permutations(), w_qkv)


def _window_attn_kernel(q_ref, k_ref, v_ref, bias_ref, o_ref, lse_ref, kpad_ref, vpad_ref, *,
                        sub_len, dilation, n_res):
    half = Q_BLOCK_A // 2
    res_blk = pl.program_id(2)
    n_q = sub_len // Q_BLOCK_A
    scale = HEAD_DIM_A ** -0.5
    c = scale * math.log2(math.e)

    zeros = jnp.zeros((half, HEAD_DIM_A), BF16)
    for rr in range(n_res):
        kpad_ref[rr, 0:half, :] = zeros
        kpad_ref[rr, half + sub_len:2 * half + sub_len, :] = zeros
        kpad_ref[rr, half:half + sub_len, :] = k_ref[rr]
        vpad_ref[rr, 0:half, :] = zeros
        vpad_ref[rr, half + sub_len:2 * half + sub_len, :] = zeros
        vpad_ref[rr, half:half + sub_len, :] = v_ref[rr]

    def one_block(rr, q0, edge):
        q = q_ref[rr, pl.ds(q0, Q_BLOCK_A), :]
        kw = kpad_ref[rr, pl.ds(q0, 2 * Q_BLOCK_A), :]
        vw = vpad_ref[rr, pl.ds(q0, 2 * Q_BLOCK_A), :]
        t = lax.dot_general(q, kw, (((1,), (1,)), ((), ())), preferred_element_type=F32) + bias_ref[edge]
        m = jnp.max(t, axis=-1, keepdims=True)
        p = jnp.exp2((t - m) * c)
        l = jnp.sum(p, axis=-1, keepdims=True)
        o = jnp.dot(p.astype(BF16), vw, preferred_element_type=F32) / l
        lse = m * scale + jnp.log(l)
        if dilation == 1:
            rows = pl.ds(q0, Q_BLOCK_A)
        else:
            rows = pl.ds(q0 * dilation + res_blk * n_res + rr, Q_BLOCK_A, stride=dilation)
        o_ref[rows, :] = o
        lse_ref[rows, :] = jnp.broadcast_to(lse, (Q_BLOCK_A, LANES))

    if n_q >= CHAINS_A:
        n_it = n_q // CHAINS_A

        def body(it, carry):
            for u in range(CHAINS_A):
                if u == 0:
                    edge = jnp.where(it == 0, EDGE_FIRST, 0)
                elif u == CHAINS_A - 1:
                    edge = jnp.where(it == n_it - 1, EDGE_LAST, 0)
                else:
                    edge = 0
                one_block(0, pl.multiple_of((it * CHAINS_A + u) * Q_BLOCK_A, Q_BLOCK_A), edge)
            return carry

        lax.fori_loop(0, n_it, body, 0)
    else:
        for rr in range(n_res):
            for i in range(n_q):
                one_block(rr, i * Q_BLOCK_A, (EDGE_FIRST if i == 0 else 0) | (EDGE_LAST if i == n_q - 1 else 0))


def _alibi_band_bias(group, dilation):
    n = N_GROUPS_A * HEADS_A
    head = jnp.arange(1, n + 1, dtype=F32)
    slopes = jnp.exp2(-8.0 * head / n).reshape(N_GROUPS_A, HEADS_A)[group]
    half = Q_BLOCK_A // 2
    r = np.arange(Q_BLOCK_A)[:, None]
    c = np.arange(2 * Q_BLOCK_A)[None, :]
    rel = np.abs(c - r - half)
    valid = []
    for edge in range(4):
        v = rel <= half
        if edge & EDGE_FIRST:
            v = v & (c >= half)
        if edge & EDGE_LAST:
            v = v & (c < Q_BLOCK_A + half)
        valid.append(v)
    valid = jnp.asarray(np.stack(valid))
    bias = -slopes[:, None, None, None] * jnp.asarray(rel * dilation, F32)[None, None]
    return jnp.where(valid[None], bias, NEG_INF) * (HEAD_DIM_A ** 0.5)


def _window_attn(qkv_g, group, seq_len):
    _, _, n_seq, d, sub_len, _ = qkv_g.shape
    m = n_seq * seq_len
    n_q = sub_len // Q_BLOCK_A
    n_res = 1 if n_q >= CHAINS_A else min(d, CHAINS_A // n_q)

    def in_spec(t):
        return pl.BlockSpec((None, None, None, n_res, sub_len, HEAD_DIM_A), lambda b, h, r: (t, h, b, r, 0, 0))

    out_spec = pl.BlockSpec((None, seq_len, HEAD_DIM_A), lambda b, h, r: (h, b, 0))
    out_shape = jax.ShapeDtypeStruct((HEADS_A, m, HEAD_DIM_A), F32)
    return pl.pallas_call(
        functools.partial(_window_attn_kernel, sub_len=sub_len, dilation=d, n_res=n_res),
        grid=(n_seq, HEADS_A, d // n_res),
        in_specs=[in_spec(0), in_spec(1), in_spec(2),
                  pl.BlockSpec((None, 4, Q_BLOCK_A, 2 * Q_BLOCK_A), lambda b, h, r: (h, 0, 0, 0))],
        out_specs=[out_spec, out_spec],
        out_shape=[out_shape, out_shape],
        scratch_shapes=[
            pltpu.VMEM((n_res, sub_len + Q_BLOCK_A, HEAD_DIM_A), BF16),
            pltpu.VMEM((n_res, sub_len + Q_BLOCK_A, HEAD_DIM_A), BF16),
        ],
        compiler_params=_params("parallel", "parallel", "arbitrary"),
        name=f"window_attn_g{group}",
    )(qkv_g, qkv_g, qkv_g, _alibi_band_bias(group, d))


def _mix_out_proj_kernel(x_ref, o0_ref, o1_ref, o2_ref, l0_ref, l1_ref, l2_ref, w_ref, y_ref):
    cols = []
    for h in range(HEADS_A):
        l0, l1, l2 = l0_ref[h], l1_ref[h], l2_ref[h]
        mx = jnp.maximum(jnp.maximum(l0, l1), l2)
        e0, e1, e2 = jnp.exp(l0 - mx), jnp.exp(l1 - mx), jnp.exp(l2 - mx)
        mixed = (e0 * o0_ref[h] + e1 * o1_ref[h] + e2 * o2_ref[h]) / (e0 + e1 + e2)
        cols.append(mixed.astype(BF16))
    mixed = jnp.concatenate(cols, axis=-1)
    y_ref[...] = x_ref[...] + jnp.dot(mixed, w_ref[...], preferred_element_type=F32)


def _mix_out_proj(x, outs, lses, w_o, layer, *, tm=256):
    m, d = x.shape
    row = lambda i: (i, 0)
    return pl.pallas_call(
        _mix_out_proj_kernel,
        grid=(m // tm,),
        in_specs=[pl.BlockSpec((tm, d), row)]
        + [pl.BlockSpec((HEADS_A, tm, HEAD_DIM_A), lambda i: (0, i, 0))] * 6
        + [pl.BlockSpec((None, GROUP_WIDTH_A, d), lambda i: (layer, 0, 0))],
        out_specs=pl.BlockSpec((tm, d), row),
        out_shape=jax.ShapeDtypeStruct((m, d), F32),
        compiler_params=_params("parallel"),
        name="mix_out_proj",
    )(x, *outs, *lses, w_o)


def _latent_q_kernel(c_ref, g_ref, w_ref, cos_ref, sin_ref, q_ref):
    h = _rmsnorm(c_ref[...], g_ref[...]).astype(BF16)
    full = jnp.dot(h, w_ref[...], preferred_element_type=F32)
    cos_t, sin_t = cos_ref[...], sin_ref[...]
    per = QK_NOPE_DIM + 2 * LANES
    for hd in range(N_HEADS_B):
        base = hd * per
        nope = full[:, base:base + QK_NOPE_DIM]
        rope = full[:, base + QK_NOPE_DIM:base + QK_NOPE_DIM + LANES]
        swapped = full[:, base + QK_NOPE_DIM + LANES:base + per]
        q_ref[hd, :, 0:QK_NOPE_DIM] = nope.astype(BF16)
        q_ref[hd, :, QK_NOPE_DIM:QK_PAD_B] = (rope * cos_t + swapped * sin_t).astype(BF16)


def _latent_kv_kernel(c_ref, kr_ref, krs_ref, g_ref, w_ref, cos_ref, sin_ref, k_ref, v_ref):
    h = _rmsnorm(c_ref[...], g_ref[...]).astype(BF16)
    kv = jnp.dot(h, w_ref[...], preferred_element_type=F32)
    k_rope = (kr_ref[...] * cos_ref[...] + krs_ref[...] * sin_ref[...]).astype(BF16)
    per = QK_NOPE_DIM + V_HEAD_DIM
    for hd in range(N_HEADS_B):
        k_ref[hd, :, 0:QK_NOPE_DIM] = kv[:, hd * per:hd * per + QK_NOPE_DIM].astype(BF16)
        k_ref[hd, :, QK_NOPE_DIM:QK_PAD_B] = k_rope
        v_ref[hd] = kv[:, hd * per + QK_NOPE_DIM:(hd + 1) * per].astype(BF16)


def _latent_attn_kernel(q_ref, k_ref, v_ref, o_ref, *, n_parts):
    c = (QK_NOPE_DIM + QK_ROPE_DIM) ** -0.5 * math.log2(math.e)
    k = k_ref[...]
    v = v_ref[...]
    rows = q_ref.shape[0] // n_parts
    for part in range(n_parts):
        sl = slice(part * rows, (part + 1) * rows)
        s = lax.dot_general(q_ref[sl, :], k, (((1,), (1,)), ((), ())), preferred_element_type=F32)
        m = jnp.max(s, axis=-1, keepdims=True)
        p = jnp.exp2((s - m) * c)
        l = jnp.sum(p, axis=-1, keepdims=True)
        o = jnp.dot(p.astype(BF16), v, preferred_element_type=F32)
        o_ref[sl, :] = (o / l).astype(o_ref.dtype)


def _rope_tables(seq_len):
    pos = jnp.arange(seq_len, dtype=F32)
    inv_freq = ROPE_THETA ** (-jnp.arange(0, QK_ROPE_DIM, 2, dtype=F32) / QK_ROPE_DIM)
    ang = pos[:, None] * inv_freq[None, :]
    cos, sin = jnp.cos(ang), jnp.sin(ang)
    zeros = jnp.zeros((seq_len, LANES - QK_ROPE_DIM), F32)
    return jnp.concatenate([cos, cos, zeros], axis=-1), jnp.concatenate([-sin, sin, zeros], axis=-1)


def _latent_weights(w_in, w_uq):
    half = QK_ROPE_DIM // 2
    d = w_in.shape[0]
    c_q = w_in[:, :Q_LORA_RANK]
    c_kv = w_in[:, Q_LORA_RANK:Q_LORA_RANK + KV_LORA_RANK]
    x1 = w_in[:, Q_LORA_RANK + KV_LORA_RANK:Q_LORA_RANK + KV_LORA_RANK + half]
    x2 = w_in[:, Q_LORA_RANK + KV_LORA_RANK + half:]
    zpad = jnp.zeros((d, LANES - QK_ROPE_DIM), w_in.dtype)
    w_in_p = jnp.concatenate([c_kv, x1, x2, zpad, x2, x1, zpad, c_q], axis=-1)

    w = w_uq.reshape(Q_LORA_RANK, N_HEADS_B, QK_NOPE_DIM + QK_ROPE_DIM)
    nope = w[:, :, :QK_NOPE_DIM]
    q1 = w[:, :, QK_NOPE_DIM:QK_NOPE_DIM + half]
    q2 = w[:, :, QK_NOPE_DIM + half:]
    zq = jnp.zeros((Q_LORA_RANK, N_HEADS_B, LANES - QK_ROPE_DIM), w_uq.dtype)
    w_uq_p = jnp.concatenate([nope, q1, q2, zq, q2, q1, zq], axis=-1).reshape(Q_LORA_RANK, -1)
    return w_in_p, w_uq_p


def _latent_qkv(c, q_norm, kv_norm, w_uq_p, w_ukv, cos_t, sin_t, layer, seq_len, *, tm=256):
    m = c.shape[0]
    n_pos_blocks = seq_len // tm
    row = lambda i: (i, 0)
    pos = lambda i: (i % n_pos_blocks, 0)
    const = lambda i: (0, 0)
    head_major = lambda i: (0, i, 0)
    q = pl.pallas_call(
        _latent_q_kernel,
        grid=(m // tm,),
        in_specs=[
            pl.BlockSpec((tm, Q_LORA_RANK), lambda i: (i, 1)),
            pl.BlockSpec((None, 1, Q_LORA_RANK), lambda i: (layer, 0, 0)),
            pl.BlockSpec(w_uq_p.shape, const),
            pl.BlockSpec((tm, LANES), pos),
            pl.BlockSpec((tm, LANES), pos),
        ],
        out_specs=pl.BlockSpec((N_HEADS_B, tm, QK_PAD_B), head_major),
        out_shape=jax.ShapeDtypeStruct((N_HEADS_B, m, QK_PAD_B), BF16),
        compiler_params=_params("parallel"),
        name="latent_q",
    )(c, q_norm.reshape(-1, 1, Q_LORA_RANK), w_uq_p, cos_t, sin_t)
    kr_block = KV_LORA_RANK // LANES
    k, v = pl.pallas_call(
        _latent_kv_kernel,
        grid=(m // tm,),
        in_specs=[
            pl.BlockSpec((tm, KV_LORA_RANK), row),
            pl.BlockSpec((tm, LANES), lambda i: (i, kr_block)),
            pl.BlockSpec((tm, LANES), lambda i: (i, kr_block + 1)),
            pl.BlockSpec((None, 1, KV_LORA_RANK), lambda i: (layer, 0, 0)),
            pl.BlockSpec((None,) + w_ukv.shape[1:], lambda i: (layer, 0, 0)),
            pl.BlockSpec((tm, LANES), pos),
            pl.BlockSpec((tm, LANES), pos),
        ],
        out_specs=[
            pl.BlockSpec((N_HEADS_B, tm, QK_PAD_B), head_major),
            pl.BlockSpec((N_HEADS_B, tm, V_HEAD_DIM), head_major),
        ],
        out_shape=[
            jax.ShapeDtypeStruct((N_HEADS_B, m, QK_PAD_B), BF16),
            jax.ShapeDtypeStruct((N_HEADS_B, m, V_HEAD_DIM), BF16),
        ],
        compiler_params=_params("parallel"),
        name="latent_kv",
    )(c, c, c, kv_norm.reshape(-1, 1, KV_LORA_RANK), w_ukv, cos_t, sin_t)
    return q, k, v


def _latent_attn(q, k, v, seq_len, *, part_rows=256):
    m = q.shape[1]
    n_seq = m // seq_len
    tq = min(seq_len, 2048)
    n_q = seq_len // tq
    return pl.pallas_call(
        functools.partial(_latent_attn_kernel, n_parts=tq // part_rows),
        grid=(n_seq, N_HEADS_B, n_q),
        in_specs=[
            pl.BlockSpec((None, tq, QK_PAD_B), lambda b, h, i: (h, b * n_q + i, 0)),
            pl.BlockSpec((None, seq_len, QK_PAD_B), lambda b, h, i: (h, b, 0)),
            pl.BlockSpec((None, seq_len, V_HEAD_DIM), lambda b, h, i: (h, b, 0)),
        ],
        out_specs=pl.BlockSpec((None, tq, V_HEAD_DIM), lambda b, h, i: (h, b * n_q + i, 0)),
        out_shape=jax.ShapeDtypeStruct((N_HEADS_B, m, V_HEAD_DIM), BF16),
        compiler_params=_params("parallel", "parallel", "arbitrary"),
        name="latent_attn",
    )(q, k, v)


def _out_proj_kernel(x_ref, a_ref, w_ref, y_ref):
    a = jnp.concatenate([a_ref[h] for h in range(a_ref.shape[0])], axis=-1)
    y_ref[...] = x_ref[...] + jnp.dot(a, w_ref[...], preferred_element_type=F32)


def _out_proj(x, a, w, layer, *, tm=256):
    m, d = x.shape
    n_h, _, hd = a.shape
    row = lambda i: (i, 0)
    return pl.pallas_call(
        _out_proj_kernel,
        grid=(m // tm,),
        in_specs=[
            pl.BlockSpec((tm, d), row),
            pl.BlockSpec((n_h, tm, hd), lambda i: (0, i, 0)),
            pl.BlockSpec((None, n_h * hd, d), lambda i: (layer, 0, 0)),
        ],
        out_specs=pl.BlockSpec((tm, d), row),
        out_shape=jax.ShapeDtypeStruct((m, d), F32),
        compiler_params=_params("parallel"),
        name="out_proj",
    )(x, a, w)


def _dilated_mixture_mixer(x, p, i, j, seq_len):
    qkv = _qkv_proj(x, p['mix_norm'], p['a_w_qkv'], i, j, seq_len)
    outs, lses = [], []
    for group in range(N_GROUPS_A):
        o, lse = _window_attn(qkv[group], group, seq_len)
        outs.append(o)
        lses.append(lse)
    return _mix_out_proj(x, outs, lses, p['a_w_o'], j)


def _latent_attention_mixer(x, p, i, j, seq_len):
    w_in_p, w_uq_p = p['b_latent'][j]
    c = _norm_proj(x, p['mix_norm'][i], w_in_p, F32)
    cos_t, sin_t = _rope_tables(seq_len)
    q, k, v = _latent_qkv(c, p['b_q_norm'], p['b_kv_norm'], w_uq_p, p['b_w_ukv'], cos_t, sin_t, j, seq_len)
    o = _latent_attn(q, k, v, seq_len)
    return _out_proj(x, o, p['b_w_o'], j)


def _trunk(x, seq_len, p, ffn_bf16, cast_ahead):
    depth = p['mix_norm'].shape[0]
    order = [(kind, layer) for layer in range(depth) for kind in ('ffn1', 'ffn2')]

    def ffn(x, step, final_g=None):
        kind, layer = order[step]
        g = p[kind + '_norm'][layer]
        if cast_ahead and step + 1 < len(order):
            nxt_kind, nxt_layer = order[step + 1]
            x, ffn_bf16[order[step + 1]] = _ffn(x, g, ffn_bf16[order[step]], final_g,
                                                 (*p[nxt_kind + '_f32'], nxt_layer))
            return x
        return _ffn(x, g, ffn_bf16[order[step]], final_g)

    for i in range(depth):
        x = ffn(x, 2 * i)
        j = i // 2
        if i % 2 == 0:
            x = _dilated_mixture_mixer(x, p, i, j, seq_len)
        else:
            x = _latent_attention_mixer(x, p, i, j, seq_len)
        x = ffn(x, 2 * i + 1, p['final_norm'] if i == depth - 1 else None)
    return x


def kernel(x_prompt, x_sample, ffn1_norm, ffn1_w_gate, ffn1_w_up, ffn1_w_down, mix_norm, a_w_qkv, a_w_o, b_w_in, b_q_norm, b_w_uq, b_kv_norm, b_w_ukv, b_w_o, ffn2_norm, ffn2_w_gate, ffn2_w_up, ffn2_w_down, final_norm):
    n_b = b_w_in.shape[0]
    latent = [_latent_weights(b_w_in[j], b_w_uq[j]) for j in range(n_b)]
    p = dict(
        ffn1_norm=ffn1_norm, ffn2_norm=ffn2_norm, mix_norm=mix_norm, final_norm=final_norm,
        b_q_norm=b_q_norm, b_kv_norm=b_kv_norm,
        ffn1_f32=(ffn1_w_gate, ffn1_w_up, ffn1_w_down), ffn2_f32=(ffn2_w_gate, ffn2_w_up, ffn2_w_down),
        a_w_qkv=a_w_qkv.astype(BF16), a_w_o=a_w_o.astype(BF16),
        b_latent=[(w_in_p.astype(BF16), w_uq_p.astype(BF16)) for w_in_p, w_uq_p in latent],
        b_w_ukv=b_w_ukv.astype(BF16), b_w_o=b_w_o.astype(BF16),
    )
    ffn_bf16 = {('ffn1', 0): tuple(w[0].astype(BF16) for w in p['ffn1_f32'])}
    outs = []
    for n, x in enumerate((x_prompt, x_sample)):
        b, s, d = x.shape
        outs.append(_trunk(x.reshape(b * s, d), s, p, ffn_bf16, cast_ahead=(n == 0)).reshape(b, s, d))
    return tuple(outs)
```

```python
import functools
import math

import numpy as np
import jax
import jax.numpy as jnp
from jax import lax
from jax.experimental import pallas as pl
from jax.experimental.pallas import tpu as pltpu

F32 = jnp.float32
BF16 = jnp.bfloat16

NORM_EPS = 1e-6
NEG_INF = -1e30
NORM_CHUNK_ROWS = 256
LANES = 128

DIL_GROUPS = ((128, 1), (512, 4), (2048, 16))
N_GROUPS_A = 3
HEADS_A = 8
HEAD_DIM_A = 128
GROUP_WIDTH_A = HEADS_A * HEAD_DIM_A
Q_BLOCK_A = 128
PERM_ROWS_A = 512
CHAINS_A = 8
EDGE_FIRST, EDGE_LAST = 1, 2
N_HEADS_B = 16
Q_LORA_RANK = 768
KV_LORA_RANK = 512
QK_NOPE_DIM = 128
QK_ROPE_DIM = 64
V_HEAD_DIM = 128
ROPE_THETA = 10000.0
QK_PAD_B = 256

VMEM_LIMIT_BYTES = 48 * 1024 * 1024
VMEM_LIMIT_LARGE_BYTES = 60 * 1024 * 1024


def _params(*sem, vmem_limit_bytes=VMEM_LIMIT_BYTES):
    return pltpu.CompilerParams(dimension_semantics=sem, vmem_limit_bytes=vmem_limit_bytes)


def _rmsnorm(x, g):
    ms = jnp.mean(x * x, axis=-1, keepdims=True)
    return x * lax.rsqrt(ms + NORM_EPS) * g


def _ffn_kernel(*refs, final, cast_next):
    refs = list(refs)
    x_ref, g_ref, wg_ref, wu_ref, wd_ref = refs[:5]
    del refs[:5]
    fg_ref = refs.pop(0) if final else None
    next_f32 = [refs.pop(0) for _ in range(3)] if cast_next else []
    o_ref = refs.pop(0)
    next_bf16 = [refs.pop(0) for _ in range(3)] if cast_next else []
    (h_ref,) = refs
    j = pl.program_id(1)
    last = pl.num_programs(1) - 1
    tm = x_ref.shape[0]
    chunks = [slice(r, r + NORM_CHUNK_ROWS) for r in range(0, tm, NORM_CHUNK_ROWS)]

    def partial_down(h):
        gate = jnp.dot(h, wg_ref[...], preferred_element_type=F32)
        up = jnp.dot(h, wu_ref[...], preferred_element_type=F32)
        act = (gate * jax.nn.sigmoid(gate) * up).astype(BF16)
        return jnp.dot(act, wd_ref[...], preferred_element_type=F32)

    def cast_next_tiles():
        for src, dst in zip(next_f32, next_bf16):
            dst[...] = src[...].astype(BF16)

    @pl.when(j == 0)
    def _():
        cast_next_tiles()
        for rows in chunks:
            h = _rmsnorm(x_ref[rows, :], g_ref[...]).astype(BF16)
            h_ref[rows, :] = h
            o_ref[rows, :] = partial_down(h)

    @pl.when((j > 0) & (j < last))
    def _():
        cast_next_tiles()
        o_ref[...] += partial_down(h_ref[...])

    @pl.when(j == last)
    def _():
        cast_next_tiles()
        for rows in chunks:
            y = x_ref[rows, :] + 0.5 * (o_ref[rows, :] + partial_down(h_ref[rows, :]))
            if final:
                y = _rmsnorm(y, fg_ref[...])
            o_ref[rows, :] = y


def _ffn(x, g, weights, final_g=None, cast_next=None, *, tm=1024, tf=512):
    m, d = x.shape
    wg, wu, wd = weights
    f = wg.shape[1]
    n_i, n_j = m // tm, f // tf
    assert n_j >= 2, "the kernel's first and last ff steps must be distinct"
    final = final_g is not None
    in_specs = [
        pl.BlockSpec((tm, d), lambda i, j: (i, 0)),
        pl.BlockSpec((1, d), lambda i, j: (0, 0)),
        pl.BlockSpec((d, tf), lambda i, j: (0, j)),
        pl.BlockSpec((d, tf), lambda i, j: (0, j)),
        pl.BlockSpec((tf, d), lambda i, j: (j, 0)),
    ]
    args = [x, g.reshape(1, d), wg, wu, wd]
    out_specs = [pl.BlockSpec((tm, d), lambda i, j: (i, 0))]
    out_shape = [jax.ShapeDtypeStruct((m, d), F32)]
    if final:
        in_specs.append(pl.BlockSpec((1, d), lambda i, j: (0, 0)))
        args.append(final_g.reshape(1, d))
    if cast_next is not None:
        *stacks, layer = cast_next
        td = d // n_i
        in_specs += [
            pl.BlockSpec((None, td, tf), lambda i, j: (layer, i, j)),
            pl.BlockSpec((None, td, tf), lambda i, j: (layer, i, j)),
            pl.BlockSpec((None, tf, td), lambda i, j: (layer, j, i)),
        ]
        args += stacks
        out_specs += [
            pl.BlockSpec((td, tf), lambda i, j: (i, j)),
            pl.BlockSpec((td, tf), lambda i, j: (i, j)),
            pl.BlockSpec((tf, td), lambda i, j: (j, i)),
        ]
        out_shape += [jax.ShapeDtypeStruct(w.shape[1:], BF16) for w in stacks]
    res = pl.pallas_call(
        functools.partial(_ffn_kernel, final=final, cast_next=cast_next is not None),
        grid=(n_i, n_j),
        in_specs=in_specs,
        out_specs=out_specs,
        out_shape=out_shape,
        scratch_shapes=[pltpu.VMEM((tm, d), BF16)],
        compiler_params=_params("parallel", "arbitrary", vmem_limit_bytes=VMEM_LIMIT_LARGE_BYTES),
        name="ffn",
    )(*args)
    return (res[0], tuple(res[1:])) if cast_next is not None else res[0]


def _norm_proj_kernel(x_ref, g_ref, w_ref, o_ref):
    for r in range(0, x_ref.shape[0], NORM_CHUNK_ROWS):
        rows = slice(r, r + NORM_CHUNK_ROWS)
        h = _rmsnorm(x_ref[rows, :], g_ref[...]).astype(BF16)
        o_ref[rows, :] = jnp.dot(h, w_ref[...], preferred_element_type=F32).astype(o_ref.dtype)


def _norm_proj(x, g, w, out_dtype, *, tm=512):
    m, d = x.shape
    n = w.shape[1]
    return pl.pallas_call(
        _norm_proj_kernel,
        grid=(m // tm,),
        in_specs=[
            pl.BlockSpec((tm, d), lambda i: (i, 0)),
            pl.BlockSpec((1, d), lambda i: (0, 0)),
            pl.BlockSpec((d, n), lambda i: (0, 0)),
        ],
        out_specs=pl.BlockSpec((tm, n), lambda i: (i, 0)),
        out_shape=jax.ShapeDtypeStruct((m, n), out_dtype),
        compiler_params=_params("parallel"),
        name="norm_proj",
    )(x, g.reshape(1, d), w)


def _qkv_proj_kernel(x_ref, g_ref, perm_ref, w_ref, o0_ref, o1_ref, o2_ref, hnat_ref, hperm_ref):
    j = pl.program_id(1)
    tm = x_ref.shape[0]
    n_sub = tm // PERM_ROWS_A

    def step(group, o_ref, first):
        dil = DIL_GROUPS[group][1]
        rows = PERM_ROWS_A // dil
        for sub in range(n_sub):
            sl = slice(sub * PERM_ROWS_A, (sub + 1) * PERM_ROWS_A)
            if not first:
                h = hperm_ref[sl, :]
            else:
                if group == 0:
                    h = _rmsnorm(x_ref[sl, :], g_ref[...]).astype(BF16)
                    hnat_ref[sl, :] = h
                else:
                    h = jnp.dot(perm_ref[...], hnat_ref[sl, :], preferred_element_type=F32).astype(BF16)
                hperm_ref[sl, :] = h
            res = jnp.dot(h, w_ref[...], preferred_element_type=F32)
            for hd in range(HEADS_A):
                for r in range(dil):
                    o_ref[hd, r, sub * rows:(sub + 1) * rows, :] = (
                        res[r * rows:(r + 1) * rows, hd * HEAD_DIM_A:(hd + 1) * HEAD_DIM_A].astype(BF16))

    for group, o_ref in enumerate((o0_ref, o1_ref, o2_ref)):
        pl.when(j == 3 * group)(functools.partial(step, group, o_ref, True))
        pl.when((j > 3 * group) & (j < 3 * group + 3))(functools.partial(step, group, o_ref, False))


def _residue_permutations():
    tm = PERM_ROWS_A
    mats = []
    for _, dil in DIL_GROUPS:
        rows = tm // dil
        p = np.zeros((tm, tm), np.float32)
        dst = np.arange(tm)
        src = (dst % rows) * dil + dst // rows
        p[dst, src] = 1.0
        mats.append(p)
    return jnp.asarray(np.stack(mats), BF16)


def _qkv_proj(x, g, w_qkv, g_layer, w_layer, seq_len, *, tm=1024):
    m, d_model = x.shape
    n_seq = m // seq_len
    tiles_per_seq = seq_len // tm
    n_col = 3 * N_GROUPS_A

    def out_spec(group):
        dil = DIL_GROUPS[group][1]

        def index(i, j):
            t = jnp.clip(j - 3 * group, 0, 2)
            return (t, 0, i // tiles_per_seq, 0, i % tiles_per_seq, 0)

        return pl.BlockSpec((None, HEADS_A, None, dil, tm // dil, HEAD_DIM_A), index)

    out_shape = [
        jax.ShapeDtypeStruct((3, HEADS_A, n_seq, dil, seq_len // dil, HEAD_DIM_A), BF16) for _, dil in DIL_GROUPS
    ]
    return pl.pallas_call(
        _qkv_proj_kernel,
        grid=(m // tm, n_col),
        in_specs=[
            pl.BlockSpec((tm, d_model), lambda i, j: (i, 0)),
            pl.BlockSpec((None, 1, d_model), lambda i, j: (g_layer, 0, 0)),
            pl.BlockSpec((None, PERM_ROWS_A, PERM_ROWS_A), lambda i, j: (j // 3, 0, 0)),
            pl.BlockSpec((None, d_model, GROUP_WIDTH_A), lambda i, j: (w_layer, 0, (j % 3) * N_GROUPS_A + j // 3)),
        ],
        out_specs=[out_spec(group) for group in range(N_GROUPS_A)],
        out_shape=out_shape,
        scratch_shapes=[pltpu.VMEM((tm, d_model), BF16), pltpu.VMEM((tm, d_model), BF16)],
        compiler_params=_params("parallel", "arbitrary", vmem_limit_bytes=VMEM_LIMIT_LARGE_BYTES),
        name="qkv_proj",
    )(x, g.reshape(-1, 1, d_model), _residue_permutations(), w_qkv)


def _window_attn_kernel(q_ref, k_ref, v_ref, bias_ref, o_ref, lse_ref, kpad_ref, vpad_ref, *,
                        sub_len, dilation, n_res):
    half = Q_BLOCK_A // 2
    res_blk = pl.program_id(2)
    n_q = sub_len // Q_BLOCK_A
    scale = HEAD_DIM_A ** -0.5
    c = scale * math.log2(math.e)

    zeros = jnp.zeros((half, HEAD_DIM_A), BF16)
    for rr in range(n_res):
        kpad_ref[rr, 0:half, :] = zeros
        kpad_ref[rr, half + sub_len:2 * half + sub_len, :] = zeros
        kpad_ref[rr, half:half + sub_len, :] = k_ref[rr]
        vpad_ref[rr, 0:half, :] = zeros
        vpad_ref[rr, half + sub_len:2 * half + sub_len, :] = zeros
        vpad_ref[rr, half:half + sub_len, :] = v_ref[rr]

    def one_block(rr, q0, edge):
        q = q_ref[rr, pl.ds(q0, Q_BLOCK_A), :]
        kw = kpad_ref[rr, pl.ds(q0, 2 * Q_BLOCK_A), :]
        vw = vpad_ref[rr, pl.ds(q0, 2 * Q_BLOCK_A), :]
        t = lax.dot_general(q, kw, (((1,), (1,)), ((), ())), preferred_element_type=F32) + bias_ref[edge]
        m = jnp.max(t, axis=-1, keepdims=True)
        p = jnp.exp2((t - m) * c)
        l = jnp.sum(p, axis=-1, keepdims=True)
        o = jnp.dot(p.astype(BF16), vw, preferred_element_type=F32) / l
        lse = m * scale + jnp.log(l)
        if dilation == 1:
            rows = pl.ds(q0, Q_BLOCK_A)
        else:
            rows = pl.ds(q0 * dilation + res_blk * n_res + rr, Q_BLOCK_A, stride=dilation)
        o_ref[rows, :] = o
        lse_ref[rows, :] = jnp.broadcast_to(lse, (Q_BLOCK_A, LANES))

    if n_q >= CHAINS_A:
        n_it = n_q // CHAINS_A

        def body(it, carry):
            for u in range(CHAINS_A):
                if u == 0:
                    edge = jnp.where(it == 0, EDGE_FIRST, 0)
                elif u == CHAINS_A - 1:
                    edge = jnp.where(it == n_it - 1, EDGE_LAST, 0)
                else:
                    edge = 0
                one_block(0, pl.multiple_of((it * CHAINS_A + u) * Q_BLOCK_A, Q_BLOCK_A), edge)
            return carry

        lax.fori_loop(0, n_it, body, 0)
    else:
        for rr in range(n_res):
            for i in range(n_q):
                one_block(rr, i * Q_BLOCK_A, (EDGE_FIRST if i == 0 else 0) | (EDGE_LAST if i == n_q - 1 else 0))


def _alibi_band_bias(group, dilation):
    n = N_GROUPS_A * HEADS_A
    head = jnp.arange(1, n + 1, dtype=F32)
    slopes = jnp.exp2(-8.0 * head / n).reshape(N_GROUPS_A, HEADS_A)[group]
    half = Q_BLOCK_A // 2
    r = np.arange(Q_BLOCK_A)[:, None]
    c = np.arange(2 * Q_BLOCK_A)[None, :]
    rel = np.abs(c - r - half)
    valid = []
    for edge in range(4):
        v = rel <= half
        if edge & EDGE_FIRST:
            v = v & (c >= half)
        if edge & EDGE_LAST:
            v = v & (c < Q_BLOCK_A + half)
        valid.append(v)
    valid = jnp.asarray(np.stack(valid))
    bias = -slopes[:, None, None, None] * jnp.asarray(rel * dilation, F32)[None, None]
    return jnp.where(valid[None], bias, NEG_INF) * (HEAD_DIM_A ** 0.5)


def _window_attn(qkv_g, group, seq_len):
    _, _, n_seq, d, sub_len, _ = qkv_g.shape
    m = n_seq * seq_len
    n_q = sub_len // Q_BLOCK_A
    n_res = 1 if n_q >= CHAINS_A else min(d, CHAINS_A // n_q)

    def in_spec(t):
        return pl.BlockSpec((None, None, None, n_res, sub_len, HEAD_DIM_A), lambda b, h, r: (t, h, b, r, 0, 0))

    out_spec = pl.BlockSpec((None, seq_len, HEAD_DIM_A), lambda b, h, r: (h, b, 0))
    out_shape = jax.ShapeDtypeStruct((HEADS_A, m, HEAD_DIM_A), F32)
    return pl.pallas_call(
        functools.partial(_window_attn_kernel, sub_len=sub_len, dilation=d, n_res=n_res),
        grid=(n_seq, HEADS_A, d // n_res),
        in_specs=[in_spec(0), in_spec(1), in_spec(2),
                  pl.BlockSpec((None, 4, Q_BLOCK_A, 2 * Q_BLOCK_A), lambda b, h, r: (h, 0, 0, 0))],
        out_specs=[out_spec, out_spec],
        out_shape=[out_shape, out_shape],
        scratch_shapes=[
            pltpu.VMEM((n_res, sub_len + Q_BLOCK_A, HEAD_DIM_A), BF16),
            pltpu.VMEM((n_res, sub_len + Q_BLOCK_A, HEAD_DIM_A), BF16),
        ],
        compiler_params=_params("parallel", "parallel", "arbitrary"),
        name=f"window_attn_g{group}",
    )(qkv_g, qkv_g, qkv_g, _alibi_band_bias(group, d))


def _mix_out_proj_kernel(x_ref, o0_ref, o1_ref, o2_ref, l0_ref, l1_ref, l2_ref, w_ref, y_ref):
    cols = []
    for h in range(HEADS_A):
        l0, l1, l2 = l0_ref[h], l1_ref[h], l2_ref[h]
        mx = jnp.maximum(jnp.maximum(l0, l1), l2)
        e0, e1, e2 = jnp.exp(l0 - mx), jnp.exp(l1 - mx), jnp.exp(l2 - mx)
        mixed = (e0 * o0_ref[h] + e1 * o1_ref[h] + e2 * o2_ref[h]) / (e0 + e1 + e2)
        cols.append(mixed.astype(BF16))
    mixed = jnp.concatenate(cols, axis=-1)
    y_ref[...] = x_ref[...] + jnp.dot(mixed, w_ref[...], preferred_element_type=F32)


def _mix_out_proj(x, outs, lses, w_o, layer, *, tm=256):
    m, d = x.shape
    row = lambda i: (i, 0)
    return pl.pallas_call(
        _mix_out_proj_kernel,
        grid=(m // tm,),
        in_specs=[pl.BlockSpec((tm, d), row)]
        + [pl.BlockSpec((HEADS_A, tm, HEAD_DIM_A), lambda i: (0, i, 0))] * 6
        + [pl.BlockSpec((None, GROUP_WIDTH_A, d), lambda i: (layer, 0, 0))],
        out_specs=pl.BlockSpec((tm, d), row),
        out_shape=jax.ShapeDtypeStruct((m, d), F32),
        compiler_params=_params("parallel"),
        name="mix_out_proj",
    )(x, *outs, *lses, w_o)


def _latent_q_kernel(c_ref, g_ref, w_ref, cos_ref, sin_ref, q_ref):
    h = _rmsnorm(c_ref[...], g_ref[...]).astype(BF16)
    full = jnp.dot(h, w_ref[...], preferred_element_type=F32)
    cos_t, sin_t = cos_ref[...], sin_ref[...]
    half = QK_ROPE_DIM // 2
    lane = lax.broadcasted_iota(jnp.int32, cos_t.shape, 1)
    for hd in range(N_HEADS_B):
        base = hd * QK_PAD_B
        rope = full[:, base + QK_NOPE_DIM:base + QK_PAD_B]
        swapped = jnp.where(lane < half, pltpu.roll(rope, LANES - half, 1), pltpu.roll(rope, half, 1))
        q_ref[hd, :, 0:QK_NOPE_DIM] = full[:, base:base + QK_NOPE_DIM].astype(BF16)
        q_ref[hd, :, QK_NOPE_DIM:QK_PAD_B] = (rope * cos_t + swapped * sin_t).astype(BF16)


def _latent_kv_kernel(c_ref, kr_ref, krs_ref, g_ref, w_ref, cos_ref, sin_ref, k_ref, v_ref):
    h = _rmsnorm(c_ref[...], g_ref[...]).astype(BF16)
    kv = jnp.dot(h, w_ref[...], preferred_element_type=F32)
    k_rope = (kr_ref[...] * cos_ref[...] + krs_ref[...] * sin_ref[...]).astype(BF16)
    per = QK_NOPE_DIM + V_HEAD_DIM
    for hd in range(N_HEADS_B):
        k_ref[hd, :, 0:QK_NOPE_DIM] = kv[:, hd * per:hd * per + QK_NOPE_DIM].astype(BF16)
        k_ref[hd, :, QK_NOPE_DIM:QK_PAD_B] = k_rope
        v_ref[hd] = kv[:, hd * per + QK_NOPE_DIM:(hd + 1) * per].astype(BF16)


def _latent_attn_kernel(q_ref, k_ref, v_ref, o_ref, *, n_parts):
    c = (QK_NOPE_DIM + QK_ROPE_DIM) ** -0.5 * math.log2(math.e)
    k = k_ref[...]
    v = v_ref[...]
    rows = q_ref.shape[0] // n_parts
    for part in range(n_parts):
        sl = slice(part * rows, (part + 1) * rows)
        s = lax.dot_general(q_ref[sl, :], k, (((1,), (1,)), ((), ())), preferred_element_type=F32)
        m = jnp.max(s, axis=-1, keepdims=True)
        p = jnp.exp2((s - m) * c)
        l = jnp.sum(p, axis=-1, keepdims=True)
        o = jnp.dot(p.astype(BF16), v, preferred_element_type=F32)
        o_ref[sl, :] = (o / l).astype(o_ref.dtype)


def _rope_tables(seq_len):
    pos = jnp.arange(seq_len, dtype=F32)
    inv_freq = ROPE_THETA ** (-jnp.arange(0, QK_ROPE_DIM, 2, dtype=F32) / QK_ROPE_DIM)
    ang = pos[:, None] * inv_freq[None, :]
    cos, sin = jnp.cos(ang), jnp.sin(ang)
    zeros = jnp.zeros((seq_len, LANES - QK_ROPE_DIM), F32)
    return jnp.concatenate([cos, cos, zeros], axis=-1), jnp.concatenate([-sin, sin, zeros], axis=-1)


def _latent_weights(w_in, w_uq):
    half = QK_ROPE_DIM // 2
    d = w_in.shape[0]
    c_q = w_in[:, :Q_LORA_RANK]
    c_kv = w_in[:, Q_LORA_RANK:Q_LORA_RANK + KV_LORA_RANK]
    x1 = w_in[:, Q_LORA_RANK + KV_LORA_RANK:Q_LORA_RANK + KV_LORA_RANK + half]
    x2 = w_in[:, Q_LORA_RANK + KV_LORA_RANK + half:]
    zpad = jnp.zeros((d, LANES - QK_ROPE_DIM), w_in.dtype)
    w_in_p = jnp.concatenate([c_kv, x1, x2, zpad, x2, x1, zpad, c_q], axis=-1)

    w = w_uq.reshape(Q_LORA_RANK, N_HEADS_B, QK_NOPE_DIM + QK_ROPE_DIM)
    nope = w[:, :, :QK_NOPE_DIM]
    q1 = w[:, :, QK_NOPE_DIM:QK_NOPE_DIM + half]
    q2 = w[:, :, QK_NOPE_DIM + half:]
    zq = jnp.zeros((Q_LORA_RANK, N_HEADS_B, LANES - QK_ROPE_DIM), w_uq.dtype)
    w_uq_p = jnp.concatenate([nope, q1, q2, zq], axis=-1).reshape(Q_LORA_RANK, -1)
    return w_in_p, w_uq_p


def _latent_qkv(c, q_norm, kv_norm, w_uq_p, w_ukv, cos_t, sin_t, layer, seq_len, *, tm=256):
    m = c.shape[0]
    n_pos_blocks = seq_len // tm
    row = lambda i: (i, 0)
    pos = lambda i: (i % n_pos_blocks, 0)
    const = lambda i: (0, 0)
    head_major = lambda i: (0, i, 0)
    q = pl.pallas_call(
        _latent_q_kernel,
        grid=(m // tm,),
        in_specs=[
            pl.BlockSpec((tm, Q_LORA_RANK), lambda i: (i, 1)),
            pl.BlockSpec((None, 1, Q_LORA_RANK), lambda i: (layer, 0, 0)),
            pl.BlockSpec(w_uq_p.shape, const),
            pl.BlockSpec((tm, LANES), pos),
            pl.BlockSpec((tm, LANES), pos),
        ],
        out_specs=pl.BlockSpec((N_HEADS_B, tm, QK_PAD_B), head_major),
        out_shape=jax.ShapeDtypeStruct((N_HEADS_B, m, QK_PAD_B), BF16),
        compiler_params=_params("parallel"),
        name="latent_q",
    )(c, q_norm.reshape(-1, 1, Q_LORA_RANK), w_uq_p, cos_t, sin_t)
    kr_block = KV_LORA_RANK // LANES
    k, v = pl.pallas_call(
        _latent_kv_kernel,
        grid=(m // tm,),
        in_specs=[
            pl.BlockSpec((tm, KV_LORA_RANK), row),
            pl.BlockSpec((tm, LANES), lambda i: (i, kr_block)),
            pl.BlockSpec((tm, LANES), lambda i: (i, kr_block + 1)),
            pl.BlockSpec((None, 1, KV_LORA_RANK), lambda i: (layer, 0, 0)),
            pl.BlockSpec((None,) + w_ukv.shape[1:], lambda i: (layer, 0, 0)),
            pl.BlockSpec((tm, LANES), pos),
            pl.BlockSpec((tm, LANES), pos),
        ],
        out_specs=[
            pl.BlockSpec((N_HEADS_B, tm, QK_PAD_B), head_major),
            pl.BlockSpec((N_HEADS_B, tm, V_HEAD_DIM), head_major),
        ],
        out_shape=[
            jax.ShapeDtypeStruct((N_HEADS_B, m, QK_PAD_B), BF16),
            jax.ShapeDtypeStruct((N_HEADS_B, m, V_HEAD_DIM), BF16),
        ],
        compiler_params=_params("parallel"),
        name="latent_kv",
    )(c, c, c, kv_norm.reshape(-1, 1, KV_LORA_RANK), w_ukv, cos_t, sin_t)
    return q, k, v


def _latent_attn(q, k, v, seq_len, *, part_rows=256):
    m = q.shape[1]
    n_seq = m // seq_len
    tq = min(seq_len, 2048)
    n_q = seq_len // tq
    return pl.pallas_call(
        functools.partial(_latent_attn_kernel, n_parts=tq // part_rows),
        grid=(n_seq, N_HEADS_B, n_q),
        in_specs=[
            pl.BlockSpec((None, tq, QK_PAD_B), lambda b, h, i: (h, b * n_q + i, 0)),
            pl.BlockSpec((None, seq_len, QK_PAD_B), lambda b, h, i: (h, b, 0)),
            pl.BlockSpec((None, seq_len, V_HEAD_DIM), lambda b, h, i: (h, b, 0)),
        ],
        out_specs=pl.BlockSpec((None, tq, V_HEAD_DIM), lambda b, h, i: (h, b * n_q + i, 0)),
        out_shape=jax.ShapeDtypeStruct((N_HEADS_B, m, V_HEAD_DIM), BF16),
        compiler_params=_params("parallel", "parallel", "arbitrary"),
        name="latent_attn",
    )(q, k, v)


def _out_proj_kernel(x_ref, a_ref, w_ref, y_ref):
    a = jnp.concatenate([a_ref[h] for h in range(a_ref.shape[0])], axis=-1)
    y_ref[...] = x_ref[...] + jnp.dot(a, w_ref[...], preferred_element_type=F32)


def _out_proj(x, a, w, layer, *, tm=256):
    m, d = x.shape
    n_h, _, hd = a.shape
    row = lambda i: (i, 0)
    return pl.pallas_call(
        _out_proj_kernel,
        grid=(m // tm,),
        in_specs=[
            pl.BlockSpec((tm, d), row),
            pl.BlockSpec((n_h, tm, hd), lambda i: (0, i, 0)),
            pl.BlockSpec((None, n_h * hd, d), lambda i: (layer, 0, 0)),
        ],
        out_specs=pl.BlockSpec((tm, d), row),
        out_shape=jax.ShapeDtypeStruct((m, d), F32),
        compiler_params=_params("parallel"),
        name="out_proj",
    )(x, a, w)


def _dilated_mixture_mixer(x, p, i, j, seq_len):
    qkv = _qkv_proj(x, p['mix_norm'], p['a_w_qkv'], i, j, seq_len)
    outs, lses = [], []
    for group in range(N_GROUPS_A):
        o, lse = _window_attn(qkv[group], group, seq_len)
        outs.append(o)
        lses.append(lse)
    return _mix_out_proj(x, outs, lses, p['a_w_o'], j)


def _latent_attention_mixer(x, p, i, j, seq_len):
    w_in_p, w_uq_p = p['b_latent'][j]
    c = _norm_proj(x, p['mix_norm'][i], w_in_p, F32)
    cos_t, sin_t = _rope_tables(seq_len)
    q, k, v = _latent_qkv(c, p['b_q_norm'], p['b_kv_norm'], w_uq_p, p['b_w_ukv'], cos_t, sin_t, j, seq_len)
    o = _latent_attn(q, k, v, seq_len)
    return _out_proj(x, o, p['b_w_o'], j)


def _trunk(x, seq_len, p, ffn_bf16, cast_ahead):
    depth = p['mix_norm'].shape[0]
    order = [(kind, layer) for layer in range(depth) for kind in ('ffn1', 'ffn2')]

    def ffn(x, step, final_g=None):
        kind, layer = order[step]
        g = p[kind + '_norm'][layer]
        if cast_ahead and step + 1 < len(order):
            nxt_kind, nxt_layer = order[step + 1]
            x, ffn_bf16[order[step + 1]] = _ffn(x, g, ffn_bf16[order[step]], final_g,
                                                 (*p[nxt_kind + '_f32'], nxt_layer))
            return x
        return _ffn(x, g, ffn_bf16[order[step]], final_g)

    for i in range(depth):
        x = ffn(x, 2 * i)
        j = i // 2
        if i % 2 == 0:
            x = _dilated_mixture_mixer(x, p, i, j, seq_len)
        else:
            x = _latent_attention_mixer(x, p, i, j, seq_len)
        x = ffn(x, 2 * i + 1, p['final_norm'] if i == depth - 1 else None)
    return x


def kernel(x_prompt, x_sample, ffn1_norm, ffn1_w_gate, ffn1_w_up, ffn1_w_down, mix_norm, a_w_qkv, a_w_o, b_w_in, b_q_norm, b_w_uq, b_kv_norm, b_w_ukv, b_w_o, ffn2_norm, ffn2_w_gate, ffn2_w_up, ffn2_w_down, final_norm):
    n_b = b_w_in.shape[0]
    latent = [_latent_weights(b_w_in[j], b_w_uq[j]) for j in range(n_b)]
    p = dict(
        ffn1_norm=ffn1_norm, ffn2_norm=ffn2_norm, mix_norm=mix_norm, final_norm=final_norm,
        b_q_norm=b_q_norm, b_kv_norm=b_kv_norm,
        ffn1_f32=(ffn1_w_gate, ffn1_w_up, ffn1_w_down), ffn2_f32=(ffn2_w_gate, ffn2_w_up, ffn2_w_down),
        a_w_qkv=a_w_qkv.astype(BF16), a_w_o=a_w_o.astype(BF16),
        b_latent=[(w_in_p.astype(BF16), w_uq_p.astype(BF16)) for w_in_p, w_uq_p in latent],
        b_w_ukv=b_w_ukv.astype(BF16), b_w_o=b_w_o.astype(BF16),
    )
    ffn_bf16 = {('ffn1', 0): tuple(w[0].astype(BF16) for w in p['ffn1_f32'])}
    outs = []
    for n, x in enumerate((x_prompt, x_sample)):
        b, s, d = x.shape
        outs.append(_trunk(x.reshape(b * s, d), s, p, ffn_bf16, cast_ahead=(n == 0)).reshape(b, s, d))
    return tuple(outs)
```

```python
import functools
import math

import numpy as np
import jax
import jax.numpy as jnp
from jax import lax
from jax.experimental import pallas as pl
from jax.experimental.pallas import tpu as pltpu

F32 = jnp.float32
BF16 = jnp.bfloat16

NORM_EPS = 1e-6
NEG_INF = -1e30
NORM_CHUNK_ROWS = 256
LANES = 128

DIL_GROUPS = ((128, 1), (512, 4), (2048, 16))
N_GROUPS_A = 3
HEADS_A = 8
HEAD_DIM_A = 128
GROUP_WIDTH_A = HEADS_A * HEAD_DIM_A
Q_BLOCK_A = 128
PERM_ROWS_A = 512
CHAINS_A = 8
EDGE_FIRST, EDGE_LAST = 1, 2
N_HEADS_B = 16
Q_LORA_RANK = 768
KV_LORA_RANK = 512
QK_NOPE_DIM = 128
QK_ROPE_DIM = 64
V_HEAD_DIM = 128
ROPE_THETA = 10000.0
QK_PAD_B = 256

VMEM_LIMIT_BYTES = 48 * 1024 * 1024
VMEM_LIMIT_LARGE_BYTES = 60 * 1024 * 1024


def _params(*sem, vmem_limit_bytes=VMEM_LIMIT_BYTES):
    return pltpu.CompilerParams(dimension_semantics=sem, vmem_limit_bytes=vmem_limit_bytes)


def _rmsnorm(x, g):
    ms = jnp.mean(x * x, axis=-1, keepdims=True)
    return x * lax.rsqrt(ms + NORM_EPS) * g


def _ffn_kernel(*refs, final, cast_next):
    refs = list(refs)
    x_ref, g_ref, wg_ref, wu_ref, wd_ref = refs[:5]
    del refs[:5]
    fg_ref = refs.pop(0) if final else None
    next_f32 = [refs.pop(0) for _ in range(3)] if cast_next else []
    o_ref = refs.pop(0)
    next_bf16 = [refs.pop(0) for _ in range(3)] if cast_next else []
    (h_ref,) = refs
    j = pl.program_id(1)
    last = pl.num_programs(1) - 1
    tm = x_ref.shape[0]
    chunks = [slice(r, r + NORM_CHUNK_ROWS) for r in range(0, tm, NORM_CHUNK_ROWS)]

    def partial_down(h):
        gate = jnp.dot(h, wg_ref[...], preferred_element_type=F32)
        up = jnp.dot(h, wu_ref[...], preferred_element_type=F32)
        act = (gate * jax.nn.sigmoid(gate) * up).astype(BF16)
        return jnp.dot(act, wd_ref[...], preferred_element_type=F32)

    def cast_next_tiles():
        for src, dst in zip(next_f32, next_bf16):
            dst[...] = src[...].astype(BF16)

    @pl.when(j == 0)
    def _():
        cast_next_tiles()
        for rows in chunks:
            h = _rmsnorm(x_ref[rows, :], g_ref[...]).astype(BF16)
            h_ref[rows, :] = h
            o_ref[rows, :] = partial_down(h)

    @pl.when((j > 0) & (j < last))
    def _():
        cast_next_tiles()
        o_ref[...] += partial_down(h_ref[...])

    @pl.when(j == last)
    def _():
        cast_next_tiles()
        for rows in chunks:
            y = x_ref[rows, :] + 0.5 * (o_ref[rows, :] + partial_down(h_ref[rows, :]))
            if final:
                y = _rmsnorm(y, fg_ref[...])
            o_ref[rows, :] = y


def _ffn(x, g, weights, final_g=None, cast_next=None, *, tm=1024, tf=512):
    m, d = x.shape
    wg, wu, wd = weights
    f = wg.shape[1]
    n_i, n_j = m // tm, f // tf
    assert n_j >= 2, "the kernel's first and last ff steps must be distinct"
    final = final_g is not None
    in_specs = [
        pl.BlockSpec((tm, d), lambda i, j: (i, 0)),
        pl.BlockSpec((1, d), lambda i, j: (0, 0)),
        pl.BlockSpec((d, tf), lambda i, j: (0, j)),
        pl.BlockSpec((d, tf), lambda i, j: (0, j)),
        pl.BlockSpec((tf, d), lambda i, j: (j, 0)),
    ]
    args = [x, g.reshape(1, d), wg, wu, wd]
    out_specs = [pl.BlockSpec((tm, d), lambda i, j: (i, 0))]
    out_shape = [jax.ShapeDtypeStruct((m, d), F32)]
    if final:
        in_specs.append(pl.BlockSpec((1, d), lambda i, j: (0, 0)))
        args.append(final_g.reshape(1, d))
    if cast_next is not None:
        *stacks, layer = cast_next
        td = d // n_i
        in_specs += [
            pl.BlockSpec((None, td, tf), lambda i, j: (layer, i, j)),
            pl.BlockSpec((None, td, tf), lambda i, j: (layer, i, j)),
            pl.BlockSpec((None, tf, td), lambda i, j: (layer, j, i)),
        ]
        args += stacks
        out_specs += [
            pl.BlockSpec((td, tf), lambda i, j: (i, j)),
            pl.BlockSpec((td, tf), lambda i, j: (i, j)),
            pl.BlockSpec((tf, td), lambda i, j: (j, i)),
        ]
        out_shape += [jax.ShapeDtypeStruct(w.shape[1:], BF16) for w in stacks]
    res = pl.pallas_call(
        functools.partial(_ffn_kernel, final=final, cast_next=cast_next is not None),
        grid=(n_i, n_j),
        in_specs=in_specs,
        out_specs=out_specs,
        out_shape=out_shape,
        scratch_shapes=[pltpu.VMEM((tm, d), BF16)],
        compiler_params=_params("parallel", "arbitrary", vmem_limit_bytes=VMEM_LIMIT_LARGE_BYTES),
        name="ffn",
    )(*args)
    return (res[0], tuple(res[1:])) if cast_next is not None else res[0]


def _norm_proj_kernel(x_ref, g_ref, w_ref, o_ref):
    for r in range(0, x_ref.shape[0], NORM_CHUNK_ROWS):
        rows = slice(r, r + NORM_CHUNK_ROWS)
        h = _rmsnorm(x_ref[rows, :], g_ref[...]).astype(BF16)
        o_ref[rows, :] = jnp.dot(h, w_ref[...], preferred_element_type=F32).astype(o_ref.dtype)


def _norm_proj(x, g, w, out_dtype, *, tm=512):
    m, d = x.shape
    n = w.shape[1]
    return pl.pallas_call(
        _norm_proj_kernel,
        grid=(m // tm,),
        in_specs=[
            pl.BlockSpec((tm, d), lambda i: (i, 0)),
            pl.BlockSpec((1, d), lambda i: (0, 0)),
            pl.BlockSpec((d, n), lambda i: (0, 0)),
        ],
        out_specs=pl.BlockSpec((tm, n), lambda i: (i, 0)),
        out_shape=jax.ShapeDtypeStruct((m, n), out_dtype),
        compiler_params=_params("parallel"),
        name="norm_proj",
    )(x, g.reshape(1, d), w)


def _qkv_proj_kernel(x_ref, g_ref, w_ref, o0_ref, o1_ref, o2_ref, hslab_ref, hperm_ref):
    j = pl.program_id(1)
    tm, d_model = x_ref.shape
    n_sub = tm // PERM_ROWS_A
    n_slab = d_model // LANES

    def step(group, o_ref, first):
        dil = DIL_GROUPS[group][1]
        rows = PERM_ROWS_A // dil
        for sub in range(n_sub):
            sl = slice(sub * PERM_ROWS_A, (sub + 1) * PERM_ROWS_A)
            if first and group == 0:
                h32 = _rmsnorm(x_ref[sl, :], g_ref[...])
                for c in range(n_slab):
                    hslab_ref[c, sl, :] = h32[:, c * LANES:(c + 1) * LANES]
                hperm_ref[sl, :] = h32.astype(BF16)
            elif first:
                for r in range(dil):
                    dst = slice(sub * PERM_ROWS_A + r * rows, sub * PERM_ROWS_A + (r + 1) * rows)
                    src = pl.ds(sub * PERM_ROWS_A + r, rows, stride=dil)
                    hperm_ref[dst, :] = jnp.concatenate(
                        [hslab_ref[c, src, :] for c in range(n_slab)], axis=-1).astype(BF16)
            res = jnp.dot(hperm_ref[sl, :], w_ref[...], preferred_element_type=F32)
            for hd in range(HEADS_A):
                for r in range(dil):
                    o_ref[hd, r, sub * rows:(sub + 1) * rows, :] = (
                        res[r * rows:(r + 1) * rows, hd * HEAD_DIM_A:(hd + 1) * HEAD_DIM_A].astype(BF16))

    for group, o_ref in enumerate((o0_ref, o1_ref, o2_ref)):
        pl.when(j == 3 * group)(functools.partial(step, group, o_ref, True))
        pl.when((j > 3 * group) & (j < 3 * group + 3))(functools.partial(step, group, o_ref, False))


def _qkv_proj(x, g, w_qkv, g_layer, w_layer, seq_len, *, tm=1024):
    m, d_model = x.shape
    n_seq = m // seq_len
    tiles_per_seq = seq_len // tm
    n_col = 3 * N_GROUPS_A

    def out_spec(group):
        dil = DIL_GROUPS[group][1]

        def index(i, j):
            t = jnp.clip(j - 3 * group, 0, 2)
            return (t, 0, i // tiles_per_seq, 0, i % tiles_per_seq, 0)

        return pl.BlockSpec((None, HEADS_A, None, dil, tm // dil, HEAD_DIM_A), index)

    out_shape = [
        jax.ShapeDtypeStruct((3, HEADS_A, n_seq, dil, seq_len // dil, HEAD_DIM_A), BF16) for _, dil in DIL_GROUPS
    ]
    return pl.pallas_call(
        _qkv_proj_kernel,
        grid=(m // tm, n_col),
        in_specs=[
            pl.BlockSpec((tm, d_model), lambda i, j: (i, 0)),
            pl.BlockSpec((None, 1, d_model), lambda i, j: (g_layer, 0, 0)),
            pl.BlockSpec((None, d_model, GROUP_WIDTH_A), lambda i, j: (w_layer, 0, (j % 3) * N_GROUPS_A + j // 3)),
        ],
        out_specs=[out_spec(group) for group in range(N_GROUPS_A)],
        out_shape=out_shape,
        scratch_shapes=[pltpu.VMEM((d_model // LANES, tm, LANES), F32), pltpu.VMEM((tm, d_model), BF16)],
        compiler_params=_params("parallel", "arbitrary", vmem_limit_bytes=VMEM_LIMIT_LARGE_BYTES),
        name="qkv_proj",
    )(x, g.reshape(-1, 1, d_model), w_qkv)


def _window_attn_kernel(q_ref, k_ref, v_ref, bias_ref, o_ref, lse_ref, kpad_ref, vpad_ref, *,
                        sub_len, dilation, n_res):
    half = Q_BLOCK_A // 2
    res_blk = pl.program_id(2)
    n_q = sub_len // Q_BLOCK_A
    scale = HEAD_DIM_A ** -0.5
    c = scale * math.log2(math.e)

    zeros = jnp.zeros((half, HEAD_DIM_A), BF16)
    for rr in range(n_res):
        kpad_ref[rr, 0:half, :] = zeros
        kpad_ref[rr, half + sub_len:2 * half + sub_len, :] = zeros
        kpad_ref[rr, half:half + sub_len, :] = k_ref[rr]
        vpad_ref[rr, 0:half, :] = zeros
        vpad_ref[rr, half + sub_len:2 * half + sub_len, :] = zeros
        vpad_ref[rr, half:half + sub_len, :] = v_ref[rr]

    def one_block(rr, q0, edge):
        q = q_ref[rr, pl.ds(q0, Q_BLOCK_A), :]
        kw = kpad_ref[rr, pl.ds(q0, 2 * Q_BLOCK_A), :]
        vw = vpad_ref[rr, pl.ds(q0, 2 * Q_BLOCK_A), :]
        t = lax.dot_general(q, kw, (((1,), (1,)), ((), ())), preferred_element_type=F32) + bias_ref[edge]
        m = jnp.max(t, axis=-1, keepdims=True)
        p = jnp.exp2((t - m) * c)
        l = jnp.sum(p, axis=-1, keepdims=True)
        o = jnp.dot(p.astype(BF16), vw, preferred_element_type=F32) / l
        lse = m * scale + jnp.log(l)
        if dilation == 1:
            rows = pl.ds(q0, Q_BLOCK_A)
        else:
            rows = pl.ds(q0 * dilation + res_blk * n_res + rr, Q_BLOCK_A, stride=dilation)
        o_ref[rows, :] = o
        lse_ref[rows, :] = jnp.broadcast_to(lse, (Q_BLOCK_A, LANES))

    if n_q >= CHAINS_A:
        n_it = n_q // CHAINS_A

        def body(it, carry):
            for u in range(CHAINS_A):
                if u == 0:
                    edge = jnp.where(it == 0, EDGE_FIRST, 0)
                elif u == CHAINS_A - 1:
                    edge = jnp.where(it == n_it - 1, EDGE_LAST, 0)
                else:
                    edge = 0
                one_block(0, pl.multiple_of((it * CHAINS_A + u) * Q_BLOCK_A, Q_BLOCK_A), edge)
            return carry

        lax.fori_loop(0, n_it, body, 0)
    else:
        for rr in range(n_res):
            for i in range(n_q):
                one_block(rr, i * Q_BLOCK_A, (EDGE_FIRST if i == 0 else 0) | (EDGE_LAST if i == n_q - 1 else 0))


def _alibi_band_bias(group, dilation):
    n = N_GROUPS_A * HEADS_A
    head = jnp.arange(1, n + 1, dtype=F32)
    slopes = jnp.exp2(-8.0 * head / n).reshape(N_GROUPS_A, HEADS_A)[group]
    half = Q_BLOCK_A // 2
    r = np.arange(Q_BLOCK_A)[:, None]
    c = np.arange(2 * Q_BLOCK_A)[None, :]
    rel = np.abs(c - r - half)
    valid = []
    for edge in range(4):
        v = rel <= half
        if edge & EDGE_FIRST:
            v = v & (c >= half)
        if edge & EDGE_LAST:
            v = v & (c < Q_BLOCK_A + half)
        valid.append(v)
    valid = jnp.asarray(np.stack(valid))
    bias = -slopes[:, None, None, None] * jnp.asarray(rel * dilation, F32)[None, None]
    return jnp.where(valid[None], bias, NEG_INF) * (HEAD_DIM_A ** 0.5)


def _window_attn(qkv_g, group, seq_len):
    _, _, n_seq, d, sub_len, _ = qkv_g.shape
    m = n_seq * seq_len
    n_q = sub_len // Q_BLOCK_A
    n_res = 1 if n_q >= CHAINS_A else min(d, CHAINS_A // n_q)

    def in_spec(t):
        return pl.BlockSpec((None, None, None, n_res, sub_len, HEAD_DIM_A), lambda b, h, r: (t, h, b, r, 0, 0))

    out_spec = pl.BlockSpec((None, seq_len, HEAD_DIM_A), lambda b, h, r: (h, b, 0))
    out_shape = jax.ShapeDtypeStruct((HEADS_A, m, HEAD_DIM_A), F32)
    return pl.pallas_call(
        functools.partial(_window_attn_kernel, sub_len=sub_len, dilation=d, n_res=n_res),
        grid=(n_seq, HEADS_A, d // n_res),
        in_specs=[in_spec(0), in_spec(1), in_spec(2),
                  pl.BlockSpec((None, 4, Q_BLOCK_A, 2 * Q_BLOCK_A), lambda b, h, r: (h, 0, 0, 0))],
        out_specs=[out_spec, out_spec],
        out_shape=[out_shape, out_shape],
        scratch_shapes=[
            pltpu.VMEM((n_res, sub_len + Q_BLOCK_A, HEAD_DIM_A), BF16),
            pltpu.VMEM((n_res, sub_len + Q_BLOCK_A, HEAD_DIM_A), BF16),
        ],
        compiler_params=_params("parallel", "parallel", "arbitrary"),
        name=f"window_attn_g{group}",
    )(qkv_g, qkv_g, qkv_g, _alibi_band_bias(group, d))


def _mix_out_proj_kernel(x_ref, o0_ref, o1_ref, o2_ref, l0_ref, l1_ref, l2_ref, w_ref, y_ref):
    cols = []
    for h in range(HEADS_A):
        l0, l1, l2 = l0_ref[h], l1_ref[h], l2_ref[h]
        mx = jnp.maximum(jnp.maximum(l0, l1), l2)
        e0, e1, e2 = jnp.exp(l0 - mx), jnp.exp(l1 - mx), jnp.exp(l2 - mx)
        mixed = (e0 * o0_ref[h] + e1 * o1_ref[h] + e2 * o2_ref[h]) / (e0 + e1 + e2)
        cols.append(mixed.astype(BF16))
    mixed = jnp.concatenate(cols, axis=-1)
    y_ref[...] = x_ref[...] + jnp.dot(mixed, w_ref[...], preferred_element_type=F32)


def _mix_out_proj(x, outs, lses, w_o, layer, *, tm=256):
    m, d = x.shape
    row = lambda i: (i, 0)
    return pl.pallas_call(
        _mix_out_proj_kernel,
        grid=(m // tm,),
        in_specs=[pl.BlockSpec((tm, d), row)]
        + [pl.BlockSpec((HEADS_A, tm, HEAD_DIM_A), lambda i: (0, i, 0))] * 6
        + [pl.BlockSpec((None, GROUP_WIDTH_A, d), lambda i: (layer, 0, 0))],
        out_specs=pl.BlockSpec((tm, d), row),
        out_shape=jax.ShapeDtypeStruct((m, d), F32),
        compiler_params=_params("parallel"),
        name="mix_out_proj",
    )(x, *outs, *lses, w_o)


def _latent_q_kernel(c_ref, g_ref, w_ref, cos_ref, sin_ref, q_ref):
    h = _rmsnorm(c_ref[...], g_ref[...]).astype(BF16)
    full = jnp.dot(h, w_ref[...], preferred_element_type=F32)
    cos_t, sin_t = cos_ref[...], sin_ref[...]
    half = QK_ROPE_DIM // 2
    lane = lax.broadcasted_iota(jnp.int32, cos_t.shape, 1)
    for hd in range(N_HEADS_B):
        base = hd * QK_PAD_B
        rope = full[:, base + QK_NOPE_DIM:base + QK_PAD_B]
        swapped = jnp.where(lane < half, pltpu.roll(rope, LANES - half, 1), pltpu.roll(rope, half, 1))
        q_ref[hd, :, 0:QK_NOPE_DIM] = full[:, base:base + QK_NOPE_DIM].astype(BF16)
        q_ref[hd, :, QK_NOPE_DIM:QK_PAD_B] = (rope * cos_t + swapped * sin_t).astype(BF16)


def _latent_kv_kernel(c_ref, kr_ref, krs_ref, g_ref, w_ref, cos_ref, sin_ref, k_ref, v_ref):
    h = _rmsnorm(c_ref[...], g_ref[...]).astype(BF16)
    kv = jnp.dot(h, w_ref[...], preferred_element_type=F32)
    k_rope = (kr_ref[...] * cos_ref[...] + krs_ref[...] * sin_ref[...]).astype(BF16)
    per = QK_NOPE_DIM + V_HEAD_DIM
    for hd in range(N_HEADS_B):
        k_ref[hd, :, 0:QK_NOPE_DIM] = kv[:, hd * per:hd * per + QK_NOPE_DIM].astype(BF16)
        k_ref[hd, :, QK_NOPE_DIM:QK_PAD_B] = k_rope
        v_ref[hd] = kv[:, hd * per + QK_NOPE_DIM:(hd + 1) * per].astype(BF16)


def _latent_attn_kernel(q_ref, k_ref, v_ref, o_ref, *, n_parts):
    c = (QK_NOPE_DIM + QK_ROPE_DIM) ** -0.5 * math.log2(math.e)
    k = k_ref[...]
    v = v_ref[...]
    rows = q_ref.shape[0] // n_parts
    for part in range(n_parts):
        sl = slice(part * rows, (part + 1) * rows)
        s = lax.dot_general(q_ref[sl, :], k, (((1,), (1,)), ((), ())), preferred_element_type=F32)
        m = jnp.max(s, axis=-1, keepdims=True)
        p = jnp.exp2((s - m) * c)
        l = jnp.sum(p, axis=-1, keepdims=True)
        o = jnp.dot(p.astype(BF16), v, preferred_element_type=F32)
        o_ref[sl, :] = (o / l).astype(o_ref.dtype)


def _rope_tables(seq_len):
    pos = jnp.arange(seq_len, dtype=F32)
    inv_freq = ROPE_THETA ** (-jnp.arange(0, QK_ROPE_DIM, 2, dtype=F32) / QK_ROPE_DIM)
    ang = pos[:, None] * inv_freq[None, :]
    cos, sin = jnp.cos(ang), jnp.sin(ang)
    zeros = jnp.zeros((seq_len, LANES - QK_ROPE_DIM), F32)
    return jnp.concatenate([cos, cos, zeros], axis=-1), jnp.concatenate([-sin, sin, zeros], axis=-1)


def _latent_weights(w_in, w_uq):
    half = QK_ROPE_DIM // 2
    d = w_in.shape[0]
    c_q = w_in[:, :Q_LORA_RANK]
    c_kv = w_in[:, Q_LORA_RANK:Q_LORA_RANK + KV_LORA_RANK]
    x1 = w_in[:, Q_LORA_RANK + KV_LORA_RANK:Q_LORA_RANK + KV_LORA_RANK + half]
    x2 = w_in[:, Q_LORA_RANK + KV_LORA_RANK + half:]
    zpad = jnp.zeros((d, LANES - QK_ROPE_DIM), w_in.dtype)
    w_in_p = jnp.concatenate([c_kv, x1, x2, zpad, x2, x1, zpad, c_q], axis=-1)

    w = w_uq.reshape(Q_LORA_RANK, N_HEADS_B, QK_NOPE_DIM + QK_ROPE_DIM)
    nope = w[:, :, :QK_NOPE_DIM]
    q1 = w[:, :, QK_NOPE_DIM:QK_NOPE_DIM + half]
    q2 = w[:, :, QK_NOPE_DIM + half:]
    zq = jnp.zeros((Q_LORA_RANK, N_HEADS_B, LANES - QK_ROPE_DIM), w_uq.dtype)
    w_uq_p = jnp.concatenate([nope, q1, q2, zq], axis=-1).reshape(Q_LORA_RANK, -1)
    return w_in_p, w_uq_p


def _latent_qkv(c, q_norm, kv_norm, w_uq_p, w_ukv, cos_t, sin_t, layer, seq_len, *, tm=256):
    m = c.shape[0]
    n_pos_blocks = seq_len // tm
    row = lambda i: (i, 0)
    pos = lambda i: (i % n_pos_blocks, 0)
    const = lambda i: (0, 0)
    head_major = lambda i: (0, i, 0)
    q = pl.pallas_call(
        _latent_q_kernel,
        grid=(m // tm,),
        in_specs=[
            pl.BlockSpec((tm, Q_LORA_RANK), lambda i: (i, 1)),
            pl.BlockSpec((None, 1, Q_LORA_RANK), lambda i: (layer, 0, 0)),
            pl.BlockSpec(w_uq_p.shape, const),
            pl.BlockSpec((tm, LANES), pos),
            pl.BlockSpec((tm, LANES), pos),
        ],
        out_specs=pl.BlockSpec((N_HEADS_B, tm, QK_PAD_B), head_major),
        out_shape=jax.ShapeDtypeStruct((N_HEADS_B, m, QK_PAD_B), BF16),
        compiler_params=_params("parallel"),
        name="latent_q",
    )(c, q_norm.reshape(-1, 1, Q_LORA_RANK), w_uq_p, cos_t, sin_t)
    kr_block = KV_LORA_RANK // LANES
    k, v = pl.pallas_call(
        _latent_kv_kernel,
        grid=(m // tm,),
        in_specs=[
            pl.BlockSpec((tm, KV_LORA_RANK), row),
            pl.BlockSpec((tm, LANES), lambda i: (i, kr_block)),
            pl.BlockSpec((tm, LANES), lambda i: (i, kr_block + 1)),
            pl.BlockSpec((None, 1, KV_LORA_RANK), lambda i: (layer, 0, 0)),
            pl.BlockSpec((None,) + w_ukv.shape[1:], lambda i: (layer, 0, 0)),
            pl.BlockSpec((tm, LANES), pos),
            pl.BlockSpec((tm, LANES), pos),
        ],
        out_specs=[
            pl.BlockSpec((N_HEADS_B, tm, QK_PAD_B), head_major),
            pl.BlockSpec((N_HEADS_B, tm, V_HEAD_DIM), head_major),
        ],
        out_shape=[
            jax.ShapeDtypeStruct((N_HEADS_B, m, QK_PAD_B), BF16),
            jax.ShapeDtypeStruct((N_HEADS_B, m, V_HEAD_DIM), BF16),
        ],
        compiler_params=_params("parallel"),
        name="latent_kv",
    )(c, c, c, kv_norm.reshape(-1, 1, KV_LORA_RANK), w_ukv, cos_t, sin_t)
    return q, k, v


def _latent_attn(q, k, v, seq_len, *, part_rows=256):
    m = q.shape[1]
    n_seq = m // seq_len
    tq = min(seq_len, 2048)
    n_q = seq_len // tq
    return pl.pallas_call(
        functools.partial(_latent_attn_kernel, n_parts=tq // part_rows),
        grid=(n_seq, N_HEADS_B, n_q),
        in_specs=[
            pl.BlockSpec((None, tq, QK_PAD_B), lambda b, h, i: (h, b * n_q + i, 0)),
            pl.BlockSpec((None, seq_len, QK_PAD_B), lambda b, h, i: (h, b, 0)),
            pl.BlockSpec((None, seq_len, V_HEAD_DIM), lambda b, h, i: (h, b, 0)),
        ],
        out_specs=pl.BlockSpec((None, tq, V_HEAD_DIM), lambda b, h, i: (h, b * n_q + i, 0)),
        out_shape=jax.ShapeDtypeStruct((N_HEADS_B, m, V_HEAD_DIM), BF16),
        compiler_params=_params("parallel", "parallel", "arbitrary"),
        name="latent_attn",
    )(q, k, v)


def _out_proj_kernel(x_ref, a_ref, w_ref, y_ref):
    a = jnp.concatenate([a_ref[h] for h in range(a_ref.shape[0])], axis=-1)
    y_ref[...] = x_ref[...] + jnp.dot(a, w_ref[...], preferred_element_type=F32)


def _out_proj(x, a, w, layer, *, tm=256):
    m, d = x.shape
    n_h, _, hd = a.shape
    row = lambda i: (i, 0)
    return pl.pallas_call(
        _out_proj_kernel,
        grid=(m // tm,),
        in_specs=[
            pl.BlockSpec((tm, d), row),
            pl.BlockSpec((n_h, tm, hd), lambda i: (0, i, 0)),
            pl.BlockSpec((None, n_h * hd, d), lambda i: (layer, 0, 0)),
        ],
        out_specs=pl.BlockSpec((tm, d), row),
        out_shape=jax.ShapeDtypeStruct((m, d), F32),
        compiler_params=_params("parallel"),
        name="out_proj",
    )(x, a, w)


def _dilated_mixture_mixer(x, p, i, j, seq_len):
    qkv = _qkv_proj(x, p['mix_norm'], p['a_w_qkv'], i, j, seq_len)
    outs, lses = [], []
    for group in range(N_GROUPS_A):
        o, lse = _window_attn(qkv[group], group, seq_len)
        outs.append(o)
        lses.append(lse)
    return _mix_out_proj(x, outs, lses, p['a_w_o'], j)


def _latent_attention_mixer(x, p, i, j, seq_len):
    w_in_p, w_uq_p = p['b_latent'][j]
    c = _norm_proj(x, p['mix_norm'][i], w_in_p, F32)
    cos_t, sin_t = _rope_tables(seq_len)
    q, k, v = _latent_qkv(c, p['b_q_norm'], p['b_kv_norm'], w_uq_p, p['b_w_ukv'], cos_t, sin_t, j, seq_len)
    o = _latent_attn(q, k, v, seq_len)
    return _out_proj(x, o, p['b_w_o'], j)


def _trunk(x, seq_len, p, ffn_bf16, cast_ahead):
    depth = p['mix_norm'].shape[0]
    order = [(kind, layer) for layer in range(depth) for kind in ('ffn1', 'ffn2')]

    def ffn(x, step, final_g=None):
        kind, layer = order[step]
        g = p[kind + '_norm'][layer]
        if cast_ahead and step + 1 < len(order):
            nxt_kind, nxt_layer = order[step + 1]
            x, ffn_bf16[order[step + 1]] = _ffn(x, g, ffn_bf16[order[step]], final_g,
                                                 (*p[nxt_kind + '_f32'], nxt_layer))
            return x
        return _ffn(x, g, ffn_bf16[order[step]], final_g)

    for i in range(depth):
        x = ffn(x, 2 * i)
        j = i // 2
        if i % 2 == 0:
            x = _dilated_mixture_mixer(x, p, i, j, seq_len)
        else:
            x = _latent_attention_mixer(x, p, i, j, seq_len)
        x = ffn(x, 2 * i + 1, p['final_norm'] if i == depth - 1 else None)
    return x


def kernel(x_prompt, x_sample, ffn1_norm, ffn1_w_gate, ffn1_w_up, ffn1_w_down, mix_norm, a_w_qkv, a_w_o, b_w_in, b_q_norm, b_w_uq, b_kv_norm, b_w_ukv, b_w_o, ffn2_norm, ffn2_w_gate, ffn2_w_up, ffn2_w_down, final_norm):
    n_b = b_w_in.shape[0]
    latent = [_latent_weights(b_w_in[j], b_w_uq[j]) for j in range(n_b)]
    p = dict(
        ffn1_norm=ffn1_norm, ffn2_norm=ffn2_norm, mix_norm=mix_norm, final_norm=final_norm,
        b_q_norm=b_q_norm, b_kv_norm=b_kv_norm,
        ffn1_f32=(ffn1_w_gate, ffn1_w_up, ffn1_w_down), ffn2_f32=(ffn2_w_gate, ffn2_w_up, ffn2_w_down),
        a_w_qkv=a_w_qkv.astype(BF16), a_w_o=a_w_o.astype(BF16),
        b_latent=[(w_in_p.astype(BF16), w_uq_p.astype(BF16)) for w_in_p, w_uq_p in latent],
        b_w_ukv=b_w_ukv.astype(BF16), b_w_o=b_w_o.astype(BF16),
    )
    ffn_bf16 = {('ffn1', 0): tuple(w[0].astype(BF16) for w in p['ffn1_f32'])}
    outs = []
    for n, x in enumerate((x_prompt, x_sample)):
        b, s, d = x.shape
        outs.append(_trunk(x.reshape(b * s, d), s, p, ffn_bf16, cast_ahead=(n == 0)).reshape(b, s, d))
    return tuple(outs)
```

```python
import functools
import math

import numpy as np
import jax
import jax.numpy as jnp
from jax import lax
from jax.experimental import pallas as pl
from jax.experimental.pallas import tpu as pltpu

F32 = jnp.float32
BF16 = jnp.bfloat16

NORM_EPS = 1e-6
NEG_INF = -1e30
NORM_CHUNK_ROWS = 256
LANES = 128

DIL_GROUPS = ((128, 1), (512, 4), (2048, 16))
N_GROUPS_A = 3
HEADS_A = 8
HEAD_DIM_A = 128
GROUP_WIDTH_A = HEADS_A * HEAD_DIM_A
Q_BLOCK_A = 128
PERM_ROWS_A = 512
CHAINS_A = 32
EDGE_FIRST, EDGE_LAST = 1, 2
N_HEADS_B = 16
Q_LORA_RANK = 768
KV_LORA_RANK = 512
QK_NOPE_DIM = 128
QK_ROPE_DIM = 64
V_HEAD_DIM = 128
ROPE_THETA = 10000.0
QK_PAD_B = 256

VMEM_LIMIT_BYTES = 48 * 1024 * 1024
VMEM_LIMIT_LARGE_BYTES = 60 * 1024 * 1024


def _params(*sem, vmem_limit_bytes=VMEM_LIMIT_BYTES):
    return pltpu.CompilerParams(dimension_semantics=sem, vmem_limit_bytes=vmem_limit_bytes)


def _rmsnorm(x, g):
    ms = jnp.mean(x * x, axis=-1, keepdims=True)
    return x * lax.rsqrt(ms + NORM_EPS) * g


def _ffn_kernel(*refs, final, cast_next):
    refs = list(refs)
    x_ref, g_ref, wg_ref, wu_ref, wd_ref = refs[:5]
    del refs[:5]
    fg_ref = refs.pop(0) if final else None
    next_f32 = [refs.pop(0) for _ in range(3)] if cast_next else []
    o_ref = refs.pop(0)
    next_bf16 = [refs.pop(0) for _ in range(3)] if cast_next else []
    (h_ref,) = refs
    j = pl.program_id(1)
    last = pl.num_programs(1) - 1
    tm = x_ref.shape[0]
    chunks = [slice(r, r + NORM_CHUNK_ROWS) for r in range(0, tm, NORM_CHUNK_ROWS)]

    def partial_down(h):
        gate = jnp.dot(h, wg_ref[...], preferred_element_type=F32)
        up = jnp.dot(h, wu_ref[...], preferred_element_type=F32)
        act = (gate * jax.nn.sigmoid(gate) * up).astype(BF16)
        return jnp.dot(act, wd_ref[...], preferred_element_type=F32)

    def cast_next_tiles():
        for src, dst in zip(next_f32, next_bf16):
            dst[...] = src[...].astype(BF16)

    @pl.when(j == 0)
    def _():
        cast_next_tiles()
        for rows in chunks:
            h = _rmsnorm(x_ref[rows, :], g_ref[...]).astype(BF16)
            h_ref[rows, :] = h
            o_ref[rows, :] = partial_down(h)

    @pl.when((j > 0) & (j < last))
    def _():
        cast_next_tiles()
        o_ref[...] += partial_down(h_ref[...])

    @pl.when(j == last)
    def _():
        cast_next_tiles()
        for rows in chunks:
            y = x_ref[rows, :] + 0.5 * (o_ref[rows, :] + partial_down(h_ref[rows, :]))
            if final:
                y = _rmsnorm(y, fg_ref[...])
            o_ref[rows, :] = y


def _ffn(x, g, weights, final_g=None, cast_next=None, *, tm=1024, tf=512):
    m, d = x.shape
    wg, wu, wd = weights
    f = wg.shape[1]
    n_i, n_j = m // tm, f // tf
    assert n_j >= 2, "the kernel's first and last ff steps must be distinct"
    final = final_g is not None
    in_specs = [
        pl.BlockSpec((tm, d), lambda i, j: (i, 0)),
        pl.BlockSpec((1, d), lambda i, j: (0, 0)),
        pl.BlockSpec((d, tf), lambda i, j: (0, j)),
        pl.BlockSpec((d, tf), lambda i, j: (0, j)),
        pl.BlockSpec((tf, d), lambda i, j: (j, 0)),
    ]
    args = [x, g.reshape(1, d), wg, wu, wd]
    out_specs = [pl.BlockSpec((tm, d), lambda i, j: (i, 0))]
    out_shape = [jax.ShapeDtypeStruct((m, d), F32)]
    if final:
        in_specs.append(pl.BlockSpec((1, d), lambda i, j: (0, 0)))
        args.append(final_g.reshape(1, d))
    if cast_next is not None:
        *stacks, layer = cast_next
        td = d // n_i
        in_specs += [
            pl.BlockSpec((None, td, tf), lambda i, j: (layer, i, j)),
            pl.BlockSpec((None, td, tf), lambda i, j: (layer, i, j)),
            pl.BlockSpec((None, tf, td), lambda i, j: (layer, j, i)),
        ]
        args += stacks
        out_specs += [
            pl.BlockSpec((td, tf), lambda i, j: (i, j)),
            pl.BlockSpec((td, tf), lambda i, j: (i, j)),
            pl.BlockSpec((tf, td), lambda i, j: (j, i)),
        ]
        out_shape += [jax.ShapeDtypeStruct(w.shape[1:], BF16) for w in stacks]
    res = pl.pallas_call(
        functools.partial(_ffn_kernel, final=final, cast_next=cast_next is not None),
        grid=(n_i, n_j),
        in_specs=in_specs,
        out_specs=out_specs,
        out_shape=out_shape,
        scratch_shapes=[pltpu.VMEM((tm, d), BF16)],
        compiler_params=_params("parallel", "arbitrary", vmem_limit_bytes=VMEM_LIMIT_LARGE_BYTES),
        name="ffn",
    )(*args)
    return (res[0], tuple(res[1:])) if cast_next is not None else res[0]


def _norm_proj_kernel(x_ref, g_ref, w_ref, o_ref):
    for r in range(0, x_ref.shape[0], NORM_CHUNK_ROWS):
        rows = slice(r, r + NORM_CHUNK_ROWS)
        h = _rmsnorm(x_ref[rows, :], g_ref[...]).astype(BF16)
        o_ref[rows, :] = jnp.dot(h, w_ref[...], preferred_element_type=F32).astype(o_ref.dtype)


def _norm_proj(x, g, w, out_dtype, *, tm=512):
    m, d = x.shape
    n = w.shape[1]
    return pl.pallas_call(
        _norm_proj_kernel,
        grid=(m // tm,),
        in_specs=[
            pl.BlockSpec((tm, d), lambda i: (i, 0)),
            pl.BlockSpec((1, d), lambda i: (0, 0)),
            pl.BlockSpec((d, n), lambda i: (0, 0)),
        ],
        out_specs=pl.BlockSpec((tm, n), lambda i: (i, 0)),
        out_shape=jax.ShapeDtypeStruct((m, n), out_dtype),
        compiler_params=_params("parallel"),
        name="norm_proj",
    )(x, g.reshape(1, d), w)


def _qkv_proj_kernel(x_ref, g_ref, w_ref, o0_ref, o1_ref, o2_ref, hslab_ref, hperm_ref):
    j = pl.program_id(1)
    tm, d_model = x_ref.shape
    n_sub = tm // PERM_ROWS_A
    n_slab = d_model // LANES

    def step(group, o_ref, first):
        dil = DIL_GROUPS[group][1]
        rows = PERM_ROWS_A // dil
        for sub in range(n_sub):
            sl = slice(sub * PERM_ROWS_A, (sub + 1) * PERM_ROWS_A)
            if first and group == 0:
                h32 = _rmsnorm(x_ref[sl, :], g_ref[...])
                for c in range(n_slab):
                    hslab_ref[c, sl, :] = h32[:, c * LANES:(c + 1) * LANES]
                hperm_ref[sl, :] = h32.astype(BF16)
            elif first:
                for r in range(dil):
                    dst = slice(sub * PERM_ROWS_A + r * rows, sub * PERM_ROWS_A + (r + 1) * rows)
                    src = pl.ds(sub * PERM_ROWS_A + r, rows, stride=dil)
                    hperm_ref[dst, :] = jnp.concatenate(
                        [hslab_ref[c, src, :] for c in range(n_slab)], axis=-1).astype(BF16)
            res = jnp.dot(hperm_ref[sl, :], w_ref[...], preferred_element_type=F32)
            for hd in range(HEADS_A):
                for r in range(dil):
                    o_ref[hd, r, sub * rows:(sub + 1) * rows, :] = (
                        res[r * rows:(r + 1) * rows, hd * HEAD_DIM_A:(hd + 1) * HEAD_DIM_A].astype(BF16))

    for group, o_ref in enumerate((o0_ref, o1_ref, o2_ref)):
        pl.when(j == 3 * group)(functools.partial(step, group, o_ref, True))
        pl.when((j > 3 * group) & (j < 3 * group + 3))(functools.partial(step, group, o_ref, False))


def _qkv_proj(x, g, w_qkv, g_layer, w_layer, seq_len, *, tm=1024):
    m, d_model = x.shape
    n_seq = m // seq_len
    tiles_per_seq = seq_len // tm
    n_col = 3 * N_GROUPS_A

    def out_spec(group):
        dil = DIL_GROUPS[group][1]

        def index(i, j):
            t = jnp.clip(j - 3 * group, 0, 2)
            return (t, 0, i // tiles_per_seq, 0, i % tiles_per_seq, 0)

        return pl.BlockSpec((None, HEADS_A, None, dil, tm // dil, HEAD_DIM_A), index)

    out_shape = [
        jax.ShapeDtypeStruct((3, HEADS_A, n_seq, dil, seq_len // dil, HEAD_DIM_A), BF16) for _, dil in DIL_GROUPS
    ]
    return pl.pallas_call(
        _qkv_proj_kernel,
        grid=(m // tm, n_col),
        in_specs=[
            pl.BlockSpec((tm, d_model), lambda i, j: (i, 0)),
            pl.BlockSpec((None, 1, d_model), lambda i, j: (g_layer, 0, 0)),
            pl.BlockSpec((None, d_model, GROUP_WIDTH_A), lambda i, j: (w_layer, 0, (j % 3) * N_GROUPS_A + j // 3)),
        ],
        out_specs=[out_spec(group) for group in range(N_GROUPS_A)],
        out_shape=out_shape,
        scratch_shapes=[pltpu.VMEM((d_model // LANES, tm, LANES), F32), pltpu.VMEM((tm, d_model), BF16)],
        compiler_params=_params("parallel", "arbitrary", vmem_limit_bytes=VMEM_LIMIT_LARGE_BYTES),
        name="qkv_proj",
    )(x, g.reshape(-1, 1, d_model), w_qkv)


def _window_attn_kernel(q_ref, k_ref, v_ref, bias_ref, o_ref, lse_ref, kpad_ref, vpad_ref, *,
                        sub_len, dilation, n_res):
    half = Q_BLOCK_A // 2
    res_blk = pl.program_id(2)
    n_q = sub_len // Q_BLOCK_A
    scale = HEAD_DIM_A ** -0.5
    c = scale * math.log2(math.e)

    zeros = jnp.zeros((half, HEAD_DIM_A), BF16)
    for rr in range(n_res):
        kpad_ref[rr, 0:half, :] = zeros
        kpad_ref[rr, half + sub_len:2 * half + sub_len, :] = zeros
        kpad_ref[rr, half:half + sub_len, :] = k_ref[rr]
        vpad_ref[rr, 0:half, :] = zeros
        vpad_ref[rr, half + sub_len:2 * half + sub_len, :] = zeros
        vpad_ref[rr, half:half + sub_len, :] = v_ref[rr]

    def one_block(rr, q0, edge):
        q = q_ref[rr, pl.ds(q0, Q_BLOCK_A), :]
        kw = kpad_ref[rr, pl.ds(q0, 2 * Q_BLOCK_A), :]
        vw = vpad_ref[rr, pl.ds(q0, 2 * Q_BLOCK_A), :]
        t = lax.dot_general(q, kw, (((1,), (1,)), ((), ())), preferred_element_type=F32) + bias_ref[edge]
        m = jnp.max(t, axis=-1, keepdims=True)
        p = jnp.exp2((t - m) * c)
        l = jnp.sum(p, axis=-1, keepdims=True)
        o = jnp.dot(p.astype(BF16), vw, preferred_element_type=F32) / l
        lse = m * scale + jnp.log(l)
        if dilation == 1:
            rows = pl.ds(q0, Q_BLOCK_A)
        else:
            rows = pl.ds(q0 * dilation + res_blk * n_res + rr, Q_BLOCK_A, stride=dilation)
        o_ref[rows, :] = o
        lse_ref[rows, :] = jnp.broadcast_to(lse, (Q_BLOCK_A, LANES))

    if n_q >= CHAINS_A:
        n_it = n_q // CHAINS_A

        def body(it, carry):
            for u in range(CHAINS_A):
                if u == 0:
                    edge = jnp.where(it == 0, EDGE_FIRST, 0)
                elif u == CHAINS_A - 1:
                    edge = jnp.where(it == n_it - 1, EDGE_LAST, 0)
                else:
                    edge = 0
                one_block(0, pl.multiple_of((it * CHAINS_A + u) * Q_BLOCK_A, Q_BLOCK_A), edge)
            return carry

        lax.fori_loop(0, n_it, body, 0)
    else:
        for rr in range(n_res):
            for i in range(n_q):
                one_block(rr, i * Q_BLOCK_A, (EDGE_FIRST if i == 0 else 0) | (EDGE_LAST if i == n_q - 1 else 0))


def _alibi_band_bias(group, dilation):
    n = N_GROUPS_A * HEADS_A
    head = jnp.arange(1, n + 1, dtype=F32)
    slopes = jnp.exp2(-8.0 * head / n).reshape(N_GROUPS_A, HEADS_A)[group]
    half = Q_BLOCK_A // 2
    r = np.arange(Q_BLOCK_A)[:, None]
    c = np.arange(2 * Q_BLOCK_A)[None, :]
    rel = np.abs(c - r - half)
    valid = []
    for edge in range(4):
        v = rel <= half
        if edge & EDGE_FIRST:
            v = v & (c >= half)
        if edge & EDGE_LAST:
            v = v & (c < Q_BLOCK_A + half)
        valid.append(v)
    valid = jnp.asarray(np.stack(valid))
    bias = -slopes[:, None, None, None] * jnp.asarray(rel * dilation, F32)[None, None]
    return jnp.where(valid[None], bias, NEG_INF) * (HEAD_DIM_A ** 0.5)


def _window_attn(qkv_g, group, seq_len):
    _, _, n_seq, d, sub_len, _ = qkv_g.shape
    m = n_seq * seq_len
    n_q = sub_len // Q_BLOCK_A
    n_res = 1 if n_q >= CHAINS_A else min(d, CHAINS_A // n_q)

    def in_spec(t):
        return pl.BlockSpec((None, None, None, n_res, sub_len, HEAD_DIM_A), lambda b, h, r: (t, h, b, r, 0, 0))

    out_spec = pl.BlockSpec((None, seq_len, HEAD_DIM_A), lambda b, h, r: (h, b, 0))
    out_shape = jax.ShapeDtypeStruct((HEADS_A, m, HEAD_DIM_A), F32)
    return pl.pallas_call(
        functools.partial(_window_attn_kernel, sub_len=sub_len, dilation=d, n_res=n_res),
        grid=(n_seq, HEADS_A, d // n_res),
        in_specs=[in_spec(0), in_spec(1), in_spec(2),
                  pl.BlockSpec((None, 4, Q_BLOCK_A, 2 * Q_BLOCK_A), lambda b, h, r: (h, 0, 0, 0))],
        out_specs=[out_spec, out_spec],
        out_shape=[out_shape, out_shape],
        scratch_shapes=[
            pltpu.VMEM((n_res, sub_len + Q_BLOCK_A, HEAD_DIM_A), BF16),
            pltpu.VMEM((n_res, sub_len + Q_BLOCK_A, HEAD_DIM_A), BF16),
        ],
        compiler_params=_params("parallel", "parallel", "arbitrary"),
        name=f"window_attn_g{group}",
    )(qkv_g, qkv_g, qkv_g, _alibi_band_bias(group, d))


def _mix_out_proj_kernel(x_ref, o0_ref, o1_ref, o2_ref, l0_ref, l1_ref, l2_ref, w_ref, y_ref):
    cols = []
    for h in range(HEADS_A):
        l0, l1, l2 = l0_ref[h], l1_ref[h], l2_ref[h]
        mx = jnp.maximum(jnp.maximum(l0, l1), l2)
        e0, e1, e2 = jnp.exp(l0 - mx), jnp.exp(l1 - mx), jnp.exp(l2 - mx)
        mixed = (e0 * o0_ref[h] + e1 * o1_ref[h] + e2 * o2_ref[h]) / (e0 + e1 + e2)
        cols.append(mixed.astype(BF16))
    mixed = jnp.concatenate(cols, axis=-1)
    y_ref[...] = x_ref[...] + jnp.dot(mixed, w_ref[...], preferred_element_type=F32)


def _mix_out_proj(x, outs, lses, w_o, layer, *, tm=256):
    m, d = x.shape
    row = lambda i: (i, 0)
    return pl.pallas_call(
        _mix_out_proj_kernel,
        grid=(m // tm,),
        in_specs=[pl.BlockSpec((tm, d), row)]
        + [pl.BlockSpec((HEADS_A, tm, HEAD_DIM_A), lambda i: (0, i, 0))] * 6
        + [pl.BlockSpec((None, GROUP_WIDTH_A, d), lambda i: (layer, 0, 0))],
        out_specs=pl.BlockSpec((tm, d), row),
        out_shape=jax.ShapeDtypeStruct((m, d), F32),
        compiler_params=_params("parallel"),
        name="mix_out_proj",
    )(x, *outs, *lses, w_o)


def _latent_q_kernel(c_ref, g_ref, w_ref, cos_ref, sin_ref, q_ref):
    h = _rmsnorm(c_ref[...], g_ref[...]).astype(BF16)
    full = jnp.dot(h, w_ref[...], preferred_element_type=F32)
    cos_t, sin_t = cos_ref[...], sin_ref[...]
    half = QK_ROPE_DIM // 2
    lane = lax.broadcasted_iota(jnp.int32, cos_t.shape, 1)
    for hd in range(N_HEADS_B):
        base = hd * QK_PAD_B
        rope = full[:, base + QK_NOPE_DIM:base + QK_PAD_B]
        swapped = jnp.where(lane < half, pltpu.roll(rope, LANES - half, 1), pltpu.roll(rope, half, 1))
        q_ref[hd, :, 0:QK_NOPE_DIM] = full[:, base:base + QK_NOPE_DIM].astype(BF16)
        q_ref[hd, :, QK_NOPE_DIM:QK_PAD_B] = (rope * cos_t + swapped * sin_t).astype(BF16)


def _latent_kv_kernel(c_ref, kr_ref, krs_ref, g_ref, w_ref, cos_ref, sin_ref, k_ref, v_ref):
    h = _rmsnorm(c_ref[...], g_ref[...]).astype(BF16)
    kv = jnp.dot(h, w_ref[...], preferred_element_type=F32)
    k_rope = (kr_ref[...] * cos_ref[...] + krs_ref[...] * sin_ref[...]).astype(BF16)
    per = QK_NOPE_DIM + V_HEAD_DIM
    for hd in range(N_HEADS_B):
        k_ref[hd, :, 0:QK_NOPE_DIM] = kv[:, hd * per:hd * per + QK_NOPE_DIM].astype(BF16)
        k_ref[hd, :, QK_NOPE_DIM:QK_PAD_B] = k_rope
        v_ref[hd] = kv[:, hd * per + QK_NOPE_DIM:(hd + 1) * per].astype(BF16)


def _latent_attn_kernel(q_ref, k_ref, v_ref, o_ref, *, n_parts):
    c = (QK_NOPE_DIM + QK_ROPE_DIM) ** -0.5 * math.log2(math.e)
    k = k_ref[...]
    v = v_ref[...]
    rows = q_ref.shape[0] // n_parts
    for part in range(n_parts):
        sl = slice(part * rows, (part + 1) * rows)
        s = lax.dot_general(q_ref[sl, :], k, (((1,), (1,)), ((), ())), preferred_element_type=F32)
        m = jnp.max(s, axis=-1, keepdims=True)
        p = jnp.exp2((s - m) * c)
        l = jnp.sum(p, axis=-1, keepdims=True)
        o = jnp.dot(p.astype(BF16), v, preferred_element_type=F32)
        o_ref[sl, :] = (o / l).astype(o_ref.dtype)


def _rope_tables(seq_len):
    pos = jnp.arange(seq_len, dtype=F32)
    inv_freq = ROPE_THETA ** (-jnp.arange(0, QK_ROPE_DIM, 2, dtype=F32) / QK_ROPE_DIM)
    ang = pos[:, None] * inv_freq[None, :]
    cos, sin = jnp.cos(ang), jnp.sin(ang)
    zeros = jnp.zeros((seq_len, LANES - QK_ROPE_DIM), F32)
    return jnp.concatenate([cos, cos, zeros], axis=-1), jnp.concatenate([-sin, sin, zeros], axis=-1)


def _latent_weights(w_in, w_uq):
    half = QK_ROPE_DIM // 2
    d = w_in.shape[0]
    c_q = w_in[:, :Q_LORA_RANK]
    c_kv = w_in[:, Q_LORA_RANK:Q_LORA_RANK + KV_LORA_RANK]
    x1 = w_in[:, Q_LORA_RANK + KV_LORA_RANK:Q_LORA_RANK + KV_LORA_RANK + half]
    x2 = w_in[:, Q_LORA_RANK + KV_LORA_RANK + half:]
    zpad = jnp.zeros((d, LANES - QK_ROPE_DIM), w_in.dtype)
    w_in_p = jnp.concatenate([c_kv, x1, x2, zpad, x2, x1, zpad, c_q], axis=-1)

    w = w_uq.reshape(Q_LORA_RANK, N_HEADS_B, QK_NOPE_DIM + QK_ROPE_DIM)
    nope = w[:, :, :QK_NOPE_DIM]
    q1 = w[:, :, QK_NOPE_DIM:QK_NOPE_DIM + half]
    q2 = w[:, :, QK_NOPE_DIM + half:]
    zq = jnp.zeros((Q_LORA_RANK, N_HEADS_B, LANES - QK_ROPE_DIM), w_uq.dtype)
    w_uq_p = jnp.concatenate([nope, q1, q2, zq], axis=-1).reshape(Q_LORA_RANK, -1)
    return w_in_p, w_uq_p


def _latent_qkv(c, q_norm, kv_norm, w_uq_p, w_ukv, cos_t, sin_t, layer, seq_len, *, tm=256):
    m = c.shape[0]
    n_pos_blocks = seq_len // tm
    row = lambda i: (i, 0)
    pos = lambda i: (i % n_pos_blocks, 0)
    const = lambda i: (0, 0)
    head_major = lambda i: (0, i, 0)
    q = pl.pallas_call(
        _latent_q_kernel,
        grid=(m // tm,),
        in_specs=[
            pl.BlockSpec((tm, Q_LORA_RANK), lambda i: (i, 1)),
            pl.BlockSpec((None, 1, Q_LORA_RANK), lambda i: (layer, 0, 0)),
            pl.BlockSpec(w_uq_p.shape, const),
            pl.BlockSpec((tm, LANES), pos),
            pl.BlockSpec((tm, LANES), pos),
        ],
        out_specs=pl.BlockSpec((N_HEADS_B, tm, QK_PAD_B), head_major),
        out_shape=jax.ShapeDtypeStruct((N_HEADS_B, m, QK_PAD_B), BF16),
        compiler_params=_params("parallel"),
        name="latent_q",
    )(c, q_norm.reshape(-1, 1, Q_LORA_RANK), w_uq_p, cos_t, sin_t)
    kr_block = KV_LORA_RANK // LANES
    k, v = pl.pallas_call(
        _latent_kv_kernel,
        grid=(m // tm,),
        in_specs=[
            pl.BlockSpec((tm, KV_LORA_RANK), row),
            pl.BlockSpec((tm, LANES), lambda i: (i, kr_block)),
            pl.BlockSpec((tm, LANES), lambda i: (i, kr_block + 1)),
            pl.BlockSpec((None, 1, KV_LORA_RANK), lambda i: (layer, 0, 0)),
            pl.BlockSpec((None,) + w_ukv.shape[1:], lambda i: (layer, 0, 0)),
            pl.BlockSpec((tm, LANES), pos),
            pl.BlockSpec((tm, LANES), pos),
        ],
        out_specs=[
            pl.BlockSpec((N_HEADS_B, tm, QK_PAD_B), head_major),
            pl.BlockSpec((N_HEADS_B, tm, V_HEAD_DIM), head_major),
        ],
        out_shape=[
            jax.ShapeDtypeStruct((N_HEADS_B, m, QK_PAD_B), BF16),
            jax.ShapeDtypeStruct((N_HEADS_B, m, V_HEAD_DIM), BF16),
        ],
        compiler_params=_params("parallel"),
        name="latent_kv",
    )(c, c, c, kv_norm.reshape(-1, 1, KV_LORA_RANK), w_ukv, cos_t, sin_t)
    return q, k, v


def _latent_attn(q, k, v, seq_len, *, part_rows=256):
    m = q.shape[1]
    n_seq = m // seq_len
    tq = min(seq_len, 2048)
    n_q = seq_len // tq
    return pl.pallas_call(
        functools.partial(_latent_attn_kernel, n_parts=tq // part_rows),
        grid=(n_seq, N_HEADS_B, n_q),
        in_specs=[
            pl.BlockSpec((None, tq, QK_PAD_B), lambda b, h, i: (h, b * n_q + i, 0)),
            pl.BlockSpec((None, seq_len, QK_PAD_B), lambda b, h, i: (h, b, 0)),
            pl.BlockSpec((None, seq_len, V_HEAD_DIM), lambda b, h, i: (h, b, 0)),
        ],
        out_specs=pl.BlockSpec((None, tq, V_HEAD_DIM), lambda b, h, i: (h, b * n_q + i, 0)),
        out_shape=jax.ShapeDtypeStruct((N_HEADS_B, m, V_HEAD_DIM), BF16),
        compiler_params=_params("parallel", "parallel", "arbitrary"),
        name="latent_attn",
    )(q, k, v)


def _out_proj_kernel(x_ref, a_ref, w_ref, y_ref):
    a = jnp.concatenate([a_ref[h] for h in range(a_ref.shape[0])], axis=-1)
    y_ref[...] = x_ref[...] + jnp.dot(a, w_ref[...], preferred_element_type=F32)


def _out_proj(x, a, w, layer, *, tm=256):
    m, d = x.shape
    n_h, _, hd = a.shape
    row = lambda i: (i, 0)
    return pl.pallas_call(
        _out_proj_kernel,
        grid=(m // tm,),
        in_specs=[
            pl.BlockSpec((tm, d), row),
            pl.BlockSpec((n_h, tm, hd), lambda i: (0, i, 0)),
            pl.BlockSpec((None, n_h * hd, d), lambda i: (layer, 0, 0)),
        ],
        out_specs=pl.BlockSpec((tm, d), row),
        out_shape=jax.ShapeDtypeStruct((m, d), F32),
        compiler_params=_params("parallel"),
        name="out_proj",
    )(x, a, w)


def _dilated_mixture_mixer(x, p, i, j, seq_len):
    qkv = _qkv_proj(x, p['mix_norm'], p['a_w_qkv'], i, j, seq_len)
    outs, lses = [], []
    for group in range(N_GROUPS_A):
        o, lse = _window_attn(qkv[group], group, seq_len)
        outs.append(o)
        lses.append(lse)
    return _mix_out_proj(x, outs, lses, p['a_w_o'], j)


def _latent_attention_mixer(x, p, i, j, seq_len):
    w_in_p, w_uq_p = p['b_latent'][j]
    c = _norm_proj(x, p['mix_norm'][i], w_in_p, F32)
    cos_t, sin_t = _rope_tables(seq_len)
    q, k, v = _latent_qkv(c, p['b_q_norm'], p['b_kv_norm'], w_uq_p, p['b_w_ukv'], cos_t, sin_t, j, seq_len)
    o = _latent_attn(q, k, v, seq_len)
    return _out_proj(x, o, p['b_w_o'], j)


def _trunk(x, seq_len, p, ffn_bf16, cast_ahead):
    depth = p['mix_norm'].shape[0]
    order = [(kind, layer) for layer in range(depth) for kind in ('ffn1', 'ffn2')]

    def ffn(x, step, final_g=None):
        kind, layer = order[step]
        g = p[kind + '_norm'][layer]
        if cast_ahead and step + 1 < len(order):
            nxt_kind, nxt_layer = order[step + 1]
            x, ffn_bf16[order[step + 1]] = _ffn(x, g, ffn_bf16[order[step]], final_g,
                                                 (*p[nxt_kind + '_f32'], nxt_layer))
            return x
        return _ffn(x, g, ffn_bf16[order[step]], final_g)

    for i in range(depth):
        x = ffn(x, 2 * i)
        j = i // 2
        if i % 2 == 0:
            x = _dilated_mixture_mixer(x, p, i, j, seq_len)
        else:
            x = _latent_attention_mixer(x, p, i, j, seq_len)
        x = ffn(x, 2 * i + 1, p['final_norm'] if i == depth - 1 else None)
    return x


def kernel(x_prompt, x_sample, ffn1_norm, ffn1_w_gate, ffn1_w_up, ffn1_w_down, mix_norm, a_w_qkv, a_w_o, b_w_in, b_q_norm, b_w_uq, b_kv_norm, b_w_ukv, b_w_o, ffn2_norm, ffn2_w_gate, ffn2_w_up, ffn2_w_down, final_norm):
    n_b = b_w_in.shape[0]
    latent = [_latent_weights(b_w_in[j], b_w_uq[j]) for j in range(n_b)]
    p = dict(
        ffn1_norm=ffn1_norm, ffn2_norm=ffn2_norm, mix_norm=mix_norm, final_norm=final_norm,
        b_q_norm=b_q_norm, b_kv_norm=b_kv_norm,
        ffn1_f32=(ffn1_w_gate, ffn1_w_up, ffn1_w_down), ffn2_f32=(ffn2_w_gate, ffn2_w_up, ffn2_w_down),
        a_w_qkv=a_w_qkv.astype(BF16), a_w_o=a_w_o.astype(BF16),
        b_latent=[(w_in_p.astype(BF16), w_uq_p.astype(BF16)) for w_in_p, w_uq_p in latent],
        b_w_ukv=b_w_ukv.astype(BF16), b_w_o=b_w_o.astype(BF16),
    )
    ffn_bf16 = {('ffn1', 0): tuple(w[0].astype(BF16) for w in p['ffn1_f32'])}
    outs = []
    for n, x in enumerate((x_prompt, x_sample)):
        b, s, d = x.shape
        outs.append(_trunk(x.reshape(b * s, d), s, p, ffn_bf16, cast_ahead=(n == 0)).reshape(b, s, d))
    return tuple(outs)
```

```python
import functools
import math

import numpy as np
import jax
import jax.numpy as jnp
from jax import lax
from jax.experimental import pallas as pl
from jax.experimental.pallas import tpu as pltpu

F32 = jnp.float32
BF16 = jnp.bfloat16

NORM_EPS = 1e-6
NEG_INF = -1e30
NORM_CHUNK_ROWS = 256
LANES = 128

DIL_GROUPS = ((128, 1), (512, 4), (2048, 16))
N_GROUPS_A = 3
HEADS_A = 8
HEAD_DIM_A = 128
GROUP_WIDTH_A = HEADS_A * HEAD_DIM_A
Q_BLOCK_A = 128
PERM_ROWS_A = 512
CHAINS_A = 32
EDGE_FIRST, EDGE_LAST = 1, 2
N_HEADS_B = 16
Q_LORA_RANK = 768
KV_LORA_RANK = 512
QK_NOPE_DIM = 128
QK_ROPE_DIM = 64
V_HEAD_DIM = 128
ROPE_THETA = 10000.0
QK_PAD_B = 256

VMEM_LIMIT_BYTES = 48 * 1024 * 1024
VMEM_LIMIT_LARGE_BYTES = 60 * 1024 * 1024


def _params(*sem, vmem_limit_bytes=VMEM_LIMIT_BYTES):
    return pltpu.CompilerParams(dimension_semantics=sem, vmem_limit_bytes=vmem_limit_bytes)


def _rmsnorm(x, g):
    ms = jnp.mean(x * x, axis=-1, keepdims=True)
    return x * lax.rsqrt(ms + NORM_EPS) * g


def _ffn_kernel(*refs, final, cast_next):
    refs = list(refs)
    x_ref, g_ref, wg_ref, wu_ref, wd_ref = refs[:5]
    del refs[:5]
    fg_ref = refs.pop(0) if final else None
    next_f32 = [refs.pop(0) for _ in range(3)] if cast_next else []
    o_ref = refs.pop(0)
    next_bf16 = [refs.pop(0) for _ in range(3)] if cast_next else []
    (h_ref,) = refs
    j = pl.program_id(1)
    last = pl.num_programs(1) - 1
    tm = x_ref.shape[0]
    chunks = [slice(r, r + NORM_CHUNK_ROWS) for r in range(0, tm, NORM_CHUNK_ROWS)]

    def partial_down(h):
        gate = jnp.dot(h, wg_ref[...], preferred_element_type=F32)
        up = jnp.dot(h, wu_ref[...], preferred_element_type=F32)
        act = (gate * jax.nn.sigmoid(gate) * up).astype(BF16)
        return jnp.dot(act, wd_ref[...], preferred_element_type=F32)

    def cast_next_tiles():
        for src, dst in zip(next_f32, next_bf16):
            dst[...] = src[...].astype(BF16)

    @pl.when(j == 0)
    def _():
        cast_next_tiles()
        for rows in chunks:
            h = _rmsnorm(x_ref[rows, :], g_ref[...]).astype(BF16)
            h_ref[rows, :] = h
            o_ref[rows, :] = partial_down(h)

    @pl.when((j > 0) & (j < last))
    def _():
        cast_next_tiles()
        o_ref[...] += partial_down(h_ref[...])

    @pl.when(j == last)
    def _():
        cast_next_tiles()
        for rows in chunks:
            y = x_ref[rows, :] + 0.5 * (o_ref[rows, :] + partial_down(h_ref[rows, :]))
            if final:
                y = _rmsnorm(y, fg_ref[...])
            o_ref[rows, :] = y


def _ffn(x, g, weights, final_g=None, cast_next=None, *, tm=1024, tf=512):
    m, d = x.shape
    wg, wu, wd = weights
    f = wg.shape[1]
    n_i, n_j = m // tm, f // tf
    assert n_j >= 2, "the kernel's first and last ff steps must be distinct"
    final = final_g is not None
    in_specs = [
        pl.BlockSpec((tm, d), lambda i, j: (i, 0)),
        pl.BlockSpec((1, d), lambda i, j: (0, 0)),
        pl.BlockSpec((d, tf), lambda i, j: (0, j)),
        pl.BlockSpec((d, tf), lambda i, j: (0, j)),
        pl.BlockSpec((tf, d), lambda i, j: (j, 0)),
    ]
    args = [x, g.reshape(1, d), wg, wu, wd]
    out_specs = [pl.BlockSpec((tm, d), lambda i, j: (i, 0))]
    out_shape = [jax.ShapeDtypeStruct((m, d), F32)]
    if final:
        in_specs.append(pl.BlockSpec((1, d), lambda i, j: (0, 0)))
        args.append(final_g.reshape(1, d))
    if cast_next is not None:
        *stacks, layer = cast_next
        td = d // n_i
        in_specs += [
            pl.BlockSpec((None, td, tf), lambda i, j: (layer, i, j)),
            pl.BlockSpec((None, td, tf), lambda i, j: (layer, i, j)),
            pl.BlockSpec((None, tf, td), lambda i, j: (layer, j, i)),
        ]
        args += stacks
        out_specs += [
            pl.BlockSpec((td, tf), lambda i, j: (i, j)),
            pl.BlockSpec((td, tf), lambda i, j: (i, j)),
            pl.BlockSpec((tf, td), lambda i, j: (j, i)),
        ]
        out_shape += [jax.ShapeDtypeStruct(w.shape[1:], BF16) for w in stacks]
    res = pl.pallas_call(
        functools.partial(_ffn_kernel, final=final, cast_next=cast_next is not None),
        grid=(n_i, n_j),
        in_specs=in_specs,
        out_specs=out_specs,
        out_shape=out_shape,
        scratch_shapes=[pltpu.VMEM((tm, d), BF16)],
        compiler_params=_params("parallel", "arbitrary", vmem_limit_bytes=VMEM_LIMIT_LARGE_BYTES),
        name="ffn",
    )(*args)
    return (res[0], tuple(res[1:])) if cast_next is not None else res[0]


def _norm_proj_kernel(x_ref, g_ref, w_ref, o_ref):
    for r in range(0, x_ref.shape[0], NORM_CHUNK_ROWS):
        rows = slice(r, r + NORM_CHUNK_ROWS)
        h = _rmsnorm(x_ref[rows, :], g_ref[...]).astype(BF16)
        o_ref[rows, :] = jnp.dot(h, w_ref[...], preferred_element_type=F32).astype(o_ref.dtype)


def _norm_proj(x, g, w, out_dtype, *, tm=512):
    m, d = x.shape
    n = w.shape[1]
    return pl.pallas_call(
        _norm_proj_kernel,
        grid=(m // tm,),
        in_specs=[
            pl.BlockSpec((tm, d), lambda i: (i, 0)),
            pl.BlockSpec((1, d), lambda i: (0, 0)),
            pl.BlockSpec((d, n), lambda i: (0, 0)),
        ],
        out_specs=pl.BlockSpec((tm, n), lambda i: (i, 0)),
        out_shape=jax.ShapeDtypeStruct((m, n), out_dtype),
        compiler_params=_params("parallel"),
        name="norm_proj",
    )(x, g.reshape(1, d), w)


def _qkv_proj_kernel(x_ref, g_ref, w_ref, o0_ref, o1_ref, o2_ref, hslab_ref, hperm_ref):
    j = pl.program_id(1)
    tm, d_model = x_ref.shape
    n_sub = tm // PERM_ROWS_A
    n_slab = d_model // LANES

    def step(group, o_ref, first):
        dil = DIL_GROUPS[group][1]
        rows = PERM_ROWS_A // dil
        for sub in range(n_sub):
            sl = slice(sub * PERM_ROWS_A, (sub + 1) * PERM_ROWS_A)
            if first and group == 0:
                h32 = _rmsnorm(x_ref[sl, :], g_ref[...])
                for c in range(n_slab):
                    hslab_ref[c, sl, :] = h32[:, c * LANES:(c + 1) * LANES]
                hperm_ref[sl, :] = h32.astype(BF16)
            elif first:
                for r in range(dil):
                    dst = slice(sub * PERM_ROWS_A + r * rows, sub * PERM_ROWS_A + (r + 1) * rows)
                    src = pl.ds(sub * PERM_ROWS_A + r, rows, stride=dil)
                    hperm_ref[dst, :] = jnp.concatenate(
                        [hslab_ref[c, src, :] for c in range(n_slab)], axis=-1).astype(BF16)
            res = jnp.dot(hperm_ref[sl, :], w_ref[...], preferred_element_type=F32)
            for hd in range(HEADS_A):
                for r in range(dil):
                    o_ref[hd, r, sub * rows:(sub + 1) * rows, :] = (
                        res[r * rows:(r + 1) * rows, hd * HEAD_DIM_A:(hd + 1) * HEAD_DIM_A].astype(BF16))

    for group, o_ref in enumerate((o0_ref, o1_ref, o2_ref)):
        pl.when(j == 3 * group)(functools.partial(step, group, o_ref, True))
        pl.when((j > 3 * group) & (j < 3 * group + 3))(functools.partial(step, group, o_ref, False))


def _qkv_proj(x, g, w_qkv, g_layer, w_layer, seq_len, *, tm=1024):
    m, d_model = x.shape
    n_seq = m // seq_len
    tiles_per_seq = seq_len // tm
    n_col = 3 * N_GROUPS_A

    def out_spec(group):
        dil = DIL_GROUPS[group][1]

        def index(i, j):
            t = jnp.clip(j - 3 * group, 0, 2)
            return (t, 0, i // tiles_per_seq, 0, i % tiles_per_seq, 0)

        return pl.BlockSpec((None, HEADS_A, None, dil, tm // dil, HEAD_DIM_A), index)

    out_shape = [
        jax.ShapeDtypeStruct((3, HEADS_A, n_seq, dil, seq_len // dil, HEAD_DIM_A), BF16) for _, dil in DIL_GROUPS
    ]
    return pl.pallas_call(
        _qkv_proj_kernel,
        grid=(m // tm, n_col),
        in_specs=[
            pl.BlockSpec((tm, d_model), lambda i, j: (i, 0)),
            pl.BlockSpec((None, 1, d_model), lambda i, j: (g_layer, 0, 0)),
            pl.BlockSpec((None, d_model, GROUP_WIDTH_A), lambda i, j: (w_layer, 0, (j % 3) * N_GROUPS_A + j // 3)),
        ],
        out_specs=[out_spec(group) for group in range(N_GROUPS_A)],
        out_shape=out_shape,
        scratch_shapes=[pltpu.VMEM((d_model // LANES, tm, LANES), F32), pltpu.VMEM((tm, d_model), BF16)],
        compiler_params=_params("parallel", "arbitrary", vmem_limit_bytes=VMEM_LIMIT_LARGE_BYTES),
        name="qkv_proj",
    )(x, g.reshape(-1, 1, d_model), w_qkv)


def _window_attn_kernel(q_ref, k_ref, v_ref, bias_ref, o_ref, lse_ref, kpad_ref, vpad_ref, *,
                        sub_len, dilation, n_res):
    half = Q_BLOCK_A // 2
    res_blk = pl.program_id(2)
    n_q = sub_len // Q_BLOCK_A
    scale = HEAD_DIM_A ** -0.5
    c = scale * math.log2(math.e)

    zeros = jnp.zeros((half, HEAD_DIM_A), BF16)
    for rr in range(n_res):
        kpad_ref[rr, 0:half, :] = zeros
        kpad_ref[rr, half + sub_len:2 * half + sub_len, :] = zeros
        kpad_ref[rr, half:half + sub_len, :] = k_ref[rr]
        vpad_ref[rr, 0:half, :] = zeros
        vpad_ref[rr, half + sub_len:2 * half + sub_len, :] = zeros
        vpad_ref[rr, half:half + sub_len, :] = v_ref[rr]

    def one_block(rr, q0, edge):
        q = q_ref[rr, pl.ds(q0, Q_BLOCK_A), :]
        kw = kpad_ref[rr, pl.ds(q0, 2 * Q_BLOCK_A), :]
        vw = vpad_ref[rr, pl.ds(q0, 2 * Q_BLOCK_A), :]
        t = lax.dot_general(q, kw, (((1,), (1,)), ((), ())), preferred_element_type=F32) + bias_ref[edge]
        m = jnp.max(t, axis=-1, keepdims=True)
        p = jnp.exp2((t - m) * c)
        l = jnp.sum(p, axis=-1, keepdims=True)
        o = jnp.dot(p.astype(BF16), vw, preferred_element_type=F32) / l
        lse = m * scale + jnp.log(l)
        if dilation == 1:
            rows = pl.ds(q0, Q_BLOCK_A)
        else:
            rows = pl.ds(q0 * dilation + res_blk * n_res + rr, Q_BLOCK_A, stride=dilation)
        o_ref[rows, :] = o
        lse_ref[rows, :] = jnp.broadcast_to(lse, (Q_BLOCK_A, LANES))

    if n_q >= CHAINS_A:
        n_it = n_q // CHAINS_A

        def body(it, carry):
            for u in range(CHAINS_A):
                if u == 0:
                    edge = jnp.where(it == 0, EDGE_FIRST, 0)
                elif u == CHAINS_A - 1:
                    edge = jnp.where(it == n_it - 1, EDGE_LAST, 0)
                else:
                    edge = 0
                one_block(0, pl.multiple_of((it * CHAINS_A + u) * Q_BLOCK_A, Q_BLOCK_A), edge)
            return carry

        lax.fori_loop(0, n_it, body, 0)
    else:
        for rr in range(n_res):
            for i in range(n_q):
                one_block(rr, i * Q_BLOCK_A, (EDGE_FIRST if i == 0 else 0) | (EDGE_LAST if i == n_q - 1 else 0))


def _alibi_band_bias(group, dilation):
    n = N_GROUPS_A * HEADS_A
    head = jnp.arange(1, n + 1, dtype=F32)
    slopes = jnp.exp2(-8.0 * head / n).reshape(N_GROUPS_A, HEADS_A)[group]
    half = Q_BLOCK_A // 2
    r = np.arange(Q_BLOCK_A)[:, None]
    c = np.arange(2 * Q_BLOCK_A)[None, :]
    rel = np.abs(c - r - half)
    valid = []
    for edge in range(4):
        v = rel <= half
        if edge & EDGE_FIRST:
            v = v & (c >= half)
        if edge & EDGE_LAST:
            v = v & (c < Q_BLOCK_A + half)
        valid.append(v)
    valid = jnp.asarray(np.stack(valid))
    bias = -slopes[:, None, None, None] * jnp.asarray(rel * dilation, F32)[None, None]
    return jnp.where(valid[None], bias, NEG_INF) * (HEAD_DIM_A ** 0.5)


def _window_attn(qkv_g, group, seq_len):
    _, _, n_seq, d, sub_len, _ = qkv_g.shape
    m = n_seq * seq_len
    n_q = sub_len // Q_BLOCK_A
    n_res = 1 if n_q >= CHAINS_A else min(d, CHAINS_A // n_q)

    def in_spec(t):
        return pl.BlockSpec((None, None, None, n_res, sub_len, HEAD_DIM_A), lambda b, h, r: (t, h, b, r, 0, 0))

    out_spec = pl.BlockSpec((None, seq_len, HEAD_DIM_A), lambda b, h, r: (h, b, 0))
    out_shape = jax.ShapeDtypeStruct((HEADS_A, m, HEAD_DIM_A), F32)
    return pl.pallas_call(
        functools.partial(_window_attn_kernel, sub_len=sub_len, dilation=d, n_res=n_res),
        grid=(n_seq, HEADS_A, d // n_res),
        in_specs=[in_spec(0), in_spec(1), in_spec(2),
                  pl.BlockSpec((None, 4, Q_BLOCK_A, 2 * Q_BLOCK_A), lambda b, h, r: (h, 0, 0, 0))],
        out_specs=[out_spec, out_spec],
        out_shape=[out_shape, out_shape],
        scratch_shapes=[
            pltpu.VMEM((n_res, sub_len + Q_BLOCK_A, HEAD_DIM_A), BF16),
            pltpu.VMEM((n_res, sub_len + Q_BLOCK_A, HEAD_DIM_A), BF16),
        ],
        compiler_params=_params("parallel", "parallel", "arbitrary"),
        name=f"window_attn_g{group}",
    )(qkv_g, qkv_g, qkv_g, _alibi_band_bias(group, d))


def _mix_out_proj_kernel(x_ref, o0_ref, o1_ref, o2_ref, l0_ref, l1_ref, l2_ref, w_ref, y_ref):
    cols = []
    for h in range(HEADS_A):
        l0, l1, l2 = l0_ref[h], l1_ref[h], l2_ref[h]
        mx = jnp.maximum(jnp.maximum(l0, l1), l2)
        e0, e1, e2 = jnp.exp(l0 - mx), jnp.exp(l1 - mx), jnp.exp(l2 - mx)
        mixed = (e0 * o0_ref[h] + e1 * o1_ref[h] + e2 * o2_ref[h]) / (e0 + e1 + e2)
        cols.append(mixed.astype(BF16))
    mixed = jnp.concatenate(cols, axis=-1)
    y_ref[...] = x_ref[...] + jnp.dot(mixed, w_ref[...], preferred_element_type=F32)


def _mix_out_proj(x, outs, lses, w_o, layer, *, tm=256):
    m, d = x.shape
    row = lambda i: (i, 0)
    return pl.pallas_call(
        _mix_out_proj_kernel,
        grid=(m // tm,),
        in_specs=[pl.BlockSpec((tm, d), row)]
        + [pl.BlockSpec((HEADS_A, tm, HEAD_DIM_A), lambda i: (0, i, 0))] * 6
        + [pl.BlockSpec((None, GROUP_WIDTH_A, d), lambda i: (layer, 0, 0))],
        out_specs=pl.BlockSpec((tm, d), row),
        out_shape=jax.ShapeDtypeStruct((m, d), F32),
        compiler_params=_params("parallel"),
        name="mix_out_proj",
    )(x, *outs, *lses, w_o)


def _latent_q_kernel(c_ref, g_ref, w_ref, cos_ref, sin_ref, q_ref):
    h = _rmsnorm(c_ref[...], g_ref[...]).astype(BF16)
    full = jnp.dot(h, w_ref[...], preferred_element_type=F32)
    half = QK_ROPE_DIM // 2
    lane = lax.broadcasted_iota(jnp.int32, cos_ref.shape, 1)
    cos_p = cos_ref[...] + pltpu.roll(cos_ref[...], QK_ROPE_DIM, 1)
    sin_p = sin_ref[...] + pltpu.roll(sin_ref[...], QK_ROPE_DIM, 1)
    first_half = lane % QK_ROPE_DIM < half
    rope_base = N_HEADS_B * QK_NOPE_DIM
    for pair in range(N_HEADS_B // 2):
        rope = full[:, rope_base + pair * LANES:rope_base + (pair + 1) * LANES]
        swapped = jnp.where(first_half, pltpu.roll(rope, LANES - half, 1), pltpu.roll(rope, half, 1))
        rot = rope * cos_p + swapped * sin_p
        for k, rot_k in enumerate((rot, pltpu.roll(rot, QK_ROPE_DIM, 1))):
            hd = 2 * pair + k
            q_ref[hd, :, 0:QK_NOPE_DIM] = full[:, hd * QK_NOPE_DIM:(hd + 1) * QK_NOPE_DIM].astype(BF16)
            q_ref[hd, :, QK_NOPE_DIM:QK_PAD_B] = jnp.where(lane < QK_ROPE_DIM, rot_k, 0.0).astype(BF16)


def _latent_kv_kernel(c_ref, kr_ref, krs_ref, g_ref, w_ref, cos_ref, sin_ref, k_ref, v_ref):
    h = _rmsnorm(c_ref[...], g_ref[...]).astype(BF16)
    kv = jnp.dot(h, w_ref[...], preferred_element_type=F32)
    k_rope = (kr_ref[...] * cos_ref[...] + krs_ref[...] * sin_ref[...]).astype(BF16)
    per = QK_NOPE_DIM + V_HEAD_DIM
    for hd in range(N_HEADS_B):
        k_ref[hd, :, 0:QK_NOPE_DIM] = kv[:, hd * per:hd * per + QK_NOPE_DIM].astype(BF16)
        k_ref[hd, :, QK_NOPE_DIM:QK_PAD_B] = k_rope
        v_ref[hd] = kv[:, hd * per + QK_NOPE_DIM:(hd + 1) * per].astype(BF16)


def _latent_attn_kernel(q_ref, k_ref, v_ref, o_ref, *, n_parts):
    c = (QK_NOPE_DIM + QK_ROPE_DIM) ** -0.5 * math.log2(math.e)
    k = k_ref[...]
    v = v_ref[...]
    rows = q_ref.shape[0] // n_parts
    for part in range(n_parts):
        sl = slice(part * rows, (part + 1) * rows)
        s = lax.dot_general(q_ref[sl, :], k, (((1,), (1,)), ((), ())), preferred_element_type=F32)
        m = jnp.max(s, axis=-1, keepdims=True)
        p = jnp.exp2((s - m) * c)
        l = jnp.sum(p, axis=-1, keepdims=True)
        o = jnp.dot(p.astype(BF16), v, preferred_element_type=F32)
        o_ref[sl, :] = (o / l).astype(o_ref.dtype)


def _rope_tables(seq_len):
    pos = jnp.arange(seq_len, dtype=F32)
    inv_freq = ROPE_THETA ** (-jnp.arange(0, QK_ROPE_DIM, 2, dtype=F32) / QK_ROPE_DIM)
    ang = pos[:, None] * inv_freq[None, :]
    cos, sin = jnp.cos(ang), jnp.sin(ang)
    zeros = jnp.zeros((seq_len, LANES - QK_ROPE_DIM), F32)
    return jnp.concatenate([cos, cos, zeros], axis=-1), jnp.concatenate([-sin, sin, zeros], axis=-1)


def _latent_weights(w_in, w_uq):
    half = QK_ROPE_DIM // 2
    d = w_in.shape[0]
    c_q = w_in[:, :Q_LORA_RANK]
    c_kv = w_in[:, Q_LORA_RANK:Q_LORA_RANK + KV_LORA_RANK]
    x1 = w_in[:, Q_LORA_RANK + KV_LORA_RANK:Q_LORA_RANK + KV_LORA_RANK + half]
    x2 = w_in[:, Q_LORA_RANK + KV_LORA_RANK + half:]
    zpad = jnp.zeros((d, LANES - QK_ROPE_DIM), w_in.dtype)
    w_in_p = jnp.concatenate([c_kv, x1, x2, zpad, x2, x1, zpad, c_q], axis=-1)

    w = w_uq.reshape(Q_LORA_RANK, N_HEADS_B, QK_NOPE_DIM + QK_ROPE_DIM)
    nope = w[:, :, :QK_NOPE_DIM].reshape(Q_LORA_RANK, -1)
    rope = w[:, :, QK_NOPE_DIM:].reshape(Q_LORA_RANK, -1)
    w_uq_p = jnp.concatenate([nope, rope], axis=-1)
    return w_in_p, w_uq_p


def _latent_qkv(c, q_norm, kv_norm, w_uq_p, w_ukv, cos_t, sin_t, layer, seq_len, *, tm=256):
    m = c.shape[0]
    n_pos_blocks = seq_len // tm
    row = lambda i: (i, 0)
    pos = lambda i: (i % n_pos_blocks, 0)
    const = lambda i: (0, 0)
    head_major = lambda i: (0, i, 0)
    q = pl.pallas_call(
        _latent_q_kernel,
        grid=(m // tm,),
        in_specs=[
            pl.BlockSpec((tm, Q_LORA_RANK), lambda i: (i, 1)),
            pl.BlockSpec((None, 1, Q_LORA_RANK), lambda i: (layer, 0, 0)),
            pl.BlockSpec(w_uq_p.shape, const),
            pl.BlockSpec((tm, LANES), pos),
            pl.BlockSpec((tm, LANES), pos),
        ],
        out_specs=pl.BlockSpec((N_HEADS_B, tm, QK_PAD_B), head_major),
        out_shape=jax.ShapeDtypeStruct((N_HEADS_B, m, QK_PAD_B), BF16),
        compiler_params=_params("parallel"),
        name="latent_q",
    )(c, q_norm.reshape(-1, 1, Q_LORA_RANK), w_uq_p, cos_t, sin_t)
    kr_block = KV_LORA_RANK // LANES
    k, v = pl.pallas_call(
        _latent_kv_kernel,
        grid=(m // tm,),
        in_specs=[
            pl.BlockSpec((tm, KV_LORA_RANK), row),
            pl.BlockSpec((tm, LANES), lambda i: (i, kr_block)),
            pl.BlockSpec((tm, LANES), lambda i: (i, kr_block + 1)),
            pl.BlockSpec((None, 1, KV_LORA_RANK), lambda i: (layer, 0, 0)),
            pl.BlockSpec((None,) + w_ukv.shape[1:], lambda i: (layer, 0, 0)),
            pl.BlockSpec((tm, LANES), pos),
            pl.BlockSpec((tm, LANES), pos),
        ],
        out_specs=[
            pl.BlockSpec((N_HEADS_B, tm, QK_PAD_B), head_major),
            pl.BlockSpec((N_HEADS_B, tm, V_HEAD_DIM), head_major),
        ],
        out_shape=[
            jax.ShapeDtypeStruct((N_HEADS_B, m, QK_PAD_B), BF16),
            jax.ShapeDtypeStruct((N_HEADS_B, m, V_HEAD_DIM), BF16),
        ],
        compiler_params=_params("parallel"),
        name="latent_kv",
    )(c, c, c, kv_norm.reshape(-1, 1, KV_LORA_RANK), w_ukv, cos_t, sin_t)
    return q, k, v


def _latent_attn(q, k, v, seq_len, *, part_rows=256):
    m = q.shape[1]
    n_seq = m // seq_len
    tq = min(seq_len, 2048)
    n_q = seq_len // tq
    return pl.pallas_call(
        functools.partial(_latent_attn_kernel, n_parts=tq // part_rows),
        grid=(n_seq, N_HEADS_B, n_q),
        in_specs=[
            pl.BlockSpec((None, tq, QK_PAD_B), lambda b, h, i: (h, b * n_q + i, 0)),
            pl.BlockSpec((None, seq_len, QK_PAD_B), lambda b, h, i: (h, b, 0)),
            pl.BlockSpec((None, seq_len, V_HEAD_DIM), lambda b, h, i: (h, b, 0)),
        ],
        out_specs=pl.BlockSpec((None, tq, V_HEAD_DIM), lambda b, h, i: (h, b * n_q + i, 0)),
        out_shape=jax.ShapeDtypeStruct((N_HEADS_B, m, V_HEAD_DIM), BF16),
        compiler_params=_params("parallel", "parallel", "arbitrary"),
        name="latent_attn",
    )(q, k, v)


def _out_proj_kernel(x_ref, a_ref, w_ref, y_ref):
    a = jnp.concatenate([a_ref[h] for h in range(a_ref.shape[0])], axis=-1)
    y_ref[...] = x_ref[...] + jnp.dot(a, w_ref[...], preferred_element_type=F32)


def _out_proj(x, a, w, layer, *, tm=256):
    m, d = x.shape
    n_h, _, hd = a.shape
    row = lambda i: (i, 0)
    return pl.pallas_call(
        _out_proj_kernel,
        grid=(m // tm,),
        in_specs=[
            pl.BlockSpec((tm, d), row),
            pl.BlockSpec((n_h, tm, hd), lambda i: (0, i, 0)),
            pl.BlockSpec((None, n_h * hd, d), lambda i: (layer, 0, 0)),
        ],
        out_specs=pl.BlockSpec((tm, d), row),
        out_shape=jax.ShapeDtypeStruct((m, d), F32),
        compiler_params=_params("parallel"),
        name="out_proj",
    )(x, a, w)


def _dilated_mixture_mixer(x, p, i, j, seq_len):
    qkv = _qkv_proj(x, p['mix_norm'], p['a_w_qkv'], i, j, seq_len)
    outs, lses = [], []
    for group in range(N_GROUPS_A):
        o, lse = _window_attn(qkv[group], group, seq_len)
        outs.append(o)
        lses.append(lse)
    return _mix_out_proj(x, outs, lses, p['a_w_o'], j)


def _latent_attention_mixer(x, p, i, j, seq_len):
    w_in_p, w_uq_p = p['b_latent'][j]
    c = _norm_proj(x, p['mix_norm'][i], w_in_p, F32)
    cos_t, sin_t = _rope_tables(seq_len)
    q, k, v = _latent_qkv(c, p['b_q_norm'], p['b_kv_norm'], w_uq_p, p['b_w_ukv'], cos_t, sin_t, j, seq_len)
    o = _latent_attn(q, k, v, seq_len)
    return _out_proj(x, o, p['b_w_o'], j)


def _trunk(x, seq_len, p, ffn_bf16, cast_ahead):
    depth = p['mix_norm'].shape[0]
    order = [(kind, layer) for layer in range(depth) for kind in ('ffn1', 'ffn2')]

    def ffn(x, step, final_g=None):
        kind, layer = order[step]
        g = p[kind + '_norm'][layer]
        if cast_ahead and step + 1 < len(order):
            nxt_kind, nxt_layer = order[step + 1]
            x, ffn_bf16[order[step + 1]] = _ffn(x, g, ffn_bf16[order[step]], final_g,
                                                 (*p[nxt_kind + '_f32'], nxt_layer))
            return x
        return _ffn(x, g, ffn_bf16[order[step]], final_g)

    for i in range(depth):
        x = ffn(x, 2 * i)
        j = i // 2
        if i % 2 == 0:
            x = _dilated_mixture_mixer(x, p, i, j, seq_len)
        else:
            x = _latent_attention_mixer(x, p, i, j, seq_len)
        x = ffn(x, 2 * i + 1, p['final_norm'] if i == depth - 1 else None)
    return x


def kernel(x_prompt, x_sample, ffn1_norm, ffn1_w_gate, ffn1_w_up, ffn1_w_down, mix_norm, a_w_qkv, a_w_o, b_w_in, b_q_norm, b_w_uq, b_kv_norm, b_w_ukv, b_w_o, ffn2_norm, ffn2_w_gate, ffn2_w_up, ffn2_w_down, final_norm):
    n_b = b_w_in.shape[0]
    latent = [_latent_weights(b_w_in[j], b_w_uq[j]) for j in range(n_b)]
    p = dict(
        ffn1_norm=ffn1_norm, ffn2_norm=ffn2_norm, mix_norm=mix_norm, final_norm=final_norm,
        b_q_norm=b_q_norm, b_kv_norm=b_kv_norm,
        ffn1_f32=(ffn1_w_gate, ffn1_w_up, ffn1_w_down), ffn2_f32=(ffn2_w_gate, ffn2_w_up, ffn2_w_down),
        a_w_qkv=a_w_qkv.astype(BF16), a_w_o=a_w_o.astype(BF16),
        b_latent=[(w_in_p.astype(BF16), w_uq_p.astype(BF16)) for w_in_p, w_uq_p in latent],
        b_w_ukv=b_w_ukv.astype(BF16), b_w_o=b_w_o.astype(BF16),
    )
    ffn_bf16 = {('ffn1', 0): tuple(w[0].astype(BF16) for w in p['ffn1_f32'])}
    outs = []
    for n, x in enumerate((x_prompt, x_sample)):
        b, s, d = x.shape
        outs.append(_trunk(x.reshape(b * s, d), s, p, ffn_bf16, cast_ahead=(n == 0)).reshape(b, s, d))
    return tuple(outs)
```

```python
import functools
import math

import numpy as np
import jax
import jax.numpy as jnp
from jax import lax
from jax.experimental import pallas as pl
from jax.experimental.pallas import tpu as pltpu

F32 = jnp.float32
BF16 = jnp.bfloat16

NORM_EPS = 1e-6
NEG_INF = -1e30
NORM_CHUNK_ROWS = 256
LANES = 128

DIL_GROUPS = ((128, 1), (512, 4), (2048, 16))
N_GROUPS_A = 3
HEADS_A = 8
HEAD_DIM_A = 128
GROUP_WIDTH_A = HEADS_A * HEAD_DIM_A
Q_BLOCK_A = 128
PERM_ROWS_A = 512
CHAINS_A = 32
EDGE_FIRST, EDGE_LAST = 1, 2
N_HEADS_B = 16
Q_LORA_RANK = 768
KV_LORA_RANK = 512
QK_NOPE_DIM = 128
QK_ROPE_DIM = 64
V_HEAD_DIM = 128
ROPE_THETA = 10000.0
QK_PAD_B = 256

VMEM_LIMIT_BYTES = 48 * 1024 * 1024
VMEM_LIMIT_LARGE_BYTES = 60 * 1024 * 1024


def _params(*sem, vmem_limit_bytes=VMEM_LIMIT_BYTES):
    return pltpu.CompilerParams(dimension_semantics=sem, vmem_limit_bytes=vmem_limit_bytes)


def _rmsnorm(x, g):
    ms = jnp.mean(x * x, axis=-1, keepdims=True)
    return x * lax.rsqrt(ms + NORM_EPS) * g


def _ffn_kernel(*refs, final, cast_next):
    refs = list(refs)
    x_ref, g_ref, wg_ref, wu_ref, wd_ref = refs[:5]
    del refs[:5]
    fg_ref = refs.pop(0) if final else None
    next_f32 = [refs.pop(0) for _ in range(3)] if cast_next else []
    o_ref = refs.pop(0)
    next_bf16 = [refs.pop(0) for _ in range(3)] if cast_next else []
    (h_ref,) = refs
    j = pl.program_id(1)
    last = pl.num_programs(1) - 1
    tm = x_ref.shape[0]
    chunks = [slice(r, r + NORM_CHUNK_ROWS) for r in range(0, tm, NORM_CHUNK_ROWS)]

    def partial_down(h):
        gate = jnp.dot(h, wg_ref[...], preferred_element_type=F32)
        up = jnp.dot(h, wu_ref[...], preferred_element_type=F32)
        act = (gate * jax.nn.sigmoid(gate) * up).astype(BF16)
        return jnp.dot(act, wd_ref[...], preferred_element_type=F32)

    def cast_next_tiles():
        for src, dst in zip(next_f32, next_bf16):
            dst[...] = src[...].astype(BF16)

    @pl.when(j == 0)
    def _():
        cast_next_tiles()
        for rows in chunks:
            h = _rmsnorm(x_ref[rows, :], g_ref[...]).astype(BF16)
            h_ref[rows, :] = h
            o_ref[rows, :] = partial_down(h)

    @pl.when((j > 0) & (j < last))
    def _():
        cast_next_tiles()
        o_ref[...] += partial_down(h_ref[...])

    @pl.when(j == last)
    def _():
        cast_next_tiles()
        for rows in chunks:
            y = x_ref[rows, :] + 0.5 * (o_ref[rows, :] + partial_down(h_ref[rows, :]))
            if final:
                y = _rmsnorm(y, fg_ref[...])
            o_ref[rows, :] = y


def _ffn(x, g, weights, final_g=None, cast_next=None, *, tm=1024, tf=512):
    m, d = x.shape
    wg, wu, wd = weights
    f = wg.shape[1]
    n_i, n_j = m // tm, f // tf
    assert n_j >= 2, "the kernel's first and last ff steps must be distinct"
    final = final_g is not None
    in_specs = [
        pl.BlockSpec((tm, d), lambda i, j: (i, 0)),
        pl.BlockSpec((1, d), lambda i, j: (0, 0)),
        pl.BlockSpec((d, tf), lambda i, j: (0, j)),
        pl.BlockSpec((d, tf), lambda i, j: (0, j)),
        pl.BlockSpec((tf, d), lambda i, j: (j, 0)),
    ]
    args = [x, g.reshape(1, d), wg, wu, wd]
    out_specs = [pl.BlockSpec((tm, d), lambda i, j: (i, 0))]
    out_shape = [jax.ShapeDtypeStruct((m, d), F32)]
    if final:
        in_specs.append(pl.BlockSpec((1, d), lambda i, j: (0, 0)))
        args.append(final_g.reshape(1, d))
    if cast_next is not None:
        *stacks, layer = cast_next
        td = d // n_i
        in_specs += [
            pl.BlockSpec((None, td, tf), lambda i, j: (layer, i, j)),
            pl.BlockSpec((None, td, tf), lambda i, j: (layer, i, j)),
            pl.BlockSpec((None, tf, td), lambda i, j: (layer, j, i)),
        ]
        args += stacks
        out_specs += [
            pl.BlockSpec((td, tf), lambda i, j: (i, j)),
            pl.BlockSpec((td, tf), lambda i, j: (i, j)),
            pl.BlockSpec((tf, td), lambda i, j: (j, i)),
        ]
        out_shape += [jax.ShapeDtypeStruct(w.shape[1:], BF16) for w in stacks]
    res = pl.pallas_call(
        functools.partial(_ffn_kernel, final=final, cast_next=cast_next is not None),
        grid=(n_i, n_j),
        in_specs=in_specs,
        out_specs=out_specs,
        out_shape=out_shape,
        scratch_shapes=[pltpu.VMEM((tm, d), BF16)],
        compiler_params=_params("parallel", "arbitrary", vmem_limit_bytes=VMEM_LIMIT_LARGE_BYTES),
        name="ffn",
    )(*args)
    return (res[0], tuple(res[1:])) if cast_next is not None else res[0]


def _norm_proj_kernel(x_ref, g_ref, w_ref, o_ref):
    for r in range(0, x_ref.shape[0], NORM_CHUNK_ROWS):
        rows = slice(r, r + NORM_CHUNK_ROWS)
        h = _rmsnorm(x_ref[rows, :], g_ref[...]).astype(BF16)
        o_ref[rows, :] = jnp.dot(h, w_ref[...], preferred_element_type=F32).astype(o_ref.dtype)


def _norm_proj(x, g, w, out_dtype, *, tm=512):
    m, d = x.shape
    n = w.shape[1]
    return pl.pallas_call(
        _norm_proj_kernel,
        grid=(m // tm,),
        in_specs=[
            pl.BlockSpec((tm, d), lambda i: (i, 0)),
            pl.BlockSpec((1, d), lambda i: (0, 0)),
            pl.BlockSpec((d, n), lambda i: (0, 0)),
        ],
        out_specs=pl.BlockSpec((tm, n), lambda i: (i, 0)),
        out_shape=jax.ShapeDtypeStruct((m, n), out_dtype),
        compiler_params=_params("parallel"),
        name="norm_proj",
    )(x, g.reshape(1, d), w)


def _qkv_proj_kernel(x_ref, g_ref, w_ref, o0_ref, o1_ref, o2_ref, hslab_ref, hperm_ref):
    j = pl.program_id(1)
    tm, d_model = x_ref.shape
    n_sub = tm // PERM_ROWS_A
    n_slab = d_model // LANES

    def step(group, o_ref, first):
        dil = DIL_GROUPS[group][1]
        rows = PERM_ROWS_A // dil
        for sub in range(n_sub):
            sl = slice(sub * PERM_ROWS_A, (sub + 1) * PERM_ROWS_A)
            if first and group == 0:
                h32 = _rmsnorm(x_ref[sl, :], g_ref[...])
                for c in range(n_slab):
                    hslab_ref[c, sl, :] = h32[:, c * LANES:(c + 1) * LANES]
                hperm_ref[sl, :] = h32.astype(BF16)
            elif first:
                for r in range(dil):
                    dst = slice(sub * PERM_ROWS_A + r * rows, sub * PERM_ROWS_A + (r + 1) * rows)
                    src = pl.ds(sub * PERM_ROWS_A + r, rows, stride=dil)
                    hperm_ref[dst, :] = jnp.concatenate(
                        [hslab_ref[c, src, :] for c in range(n_slab)], axis=-1).astype(BF16)
            res = jnp.dot(hperm_ref[sl, :], w_ref[...], preferred_element_type=F32)
            for hd in range(HEADS_A):
                for r in range(dil):
                    o_ref[hd, r, sub * rows:(sub + 1) * rows, :] = (
                        res[r * rows:(r + 1) * rows, hd * HEAD_DIM_A:(hd + 1) * HEAD_DIM_A].astype(BF16))

    for group, o_ref in enumerate((o0_ref, o1_ref, o2_ref)):
        pl.when(j == 3 * group)(functools.partial(step, group, o_ref, True))
        pl.when((j > 3 * group) & (j < 3 * group + 3))(functools.partial(step, group, o_ref, False))


def _qkv_proj(x, g, w_qkv, g_layer, w_layer, seq_len, *, tm=1024):
    m, d_model = x.shape
    n_seq = m // seq_len
    tiles_per_seq = seq_len // tm
    n_col = 3 * N_GROUPS_A

    def out_spec(group):
        dil = DIL_GROUPS[group][1]

        def index(i, j):
            t = jnp.clip(j - 3 * group, 0, 2)
            return (t, 0, i // tiles_per_seq, 0, i % tiles_per_seq, 0)

        return pl.BlockSpec((None, HEADS_A, None, dil, tm // dil, HEAD_DIM_A), index)

    out_shape = [
        jax.ShapeDtypeStruct((3, HEADS_A, n_seq, dil, seq_len // dil, HEAD_DIM_A), BF16) for _, dil in DIL_GROUPS
    ]
    return pl.pallas_call(
        _qkv_proj_kernel,
        grid=(m // tm, n_col),
        in_specs=[
            pl.BlockSpec((tm, d_model), lambda i, j: (i, 0)),
            pl.BlockSpec((None, 1, d_model), lambda i, j: (g_layer, 0, 0)),
            pl.BlockSpec((None, d_model, GROUP_WIDTH_A), lambda i, j: (w_layer, 0, (j % 3) * N_GROUPS_A + j // 3)),
        ],
        out_specs=[out_spec(group) for group in range(N_GROUPS_A)],
        out_shape=out_shape,
        scratch_shapes=[pltpu.VMEM((d_model // LANES, tm, LANES), F32), pltpu.VMEM((tm, d_model), BF16)],
        compiler_params=_params("parallel", "arbitrary", vmem_limit_bytes=VMEM_LIMIT_LARGE_BYTES),
        name="qkv_proj",
    )(x, g.reshape(-1, 1, d_model), w_qkv)


def _window_attn_kernel(q_ref, k_ref, v_ref, bias_ref, o_ref, lse_ref, kpad_ref, vpad_ref, *,
                        sub_len, dilation, n_res):
    half = Q_BLOCK_A // 2
    res_blk = pl.program_id(2)
    n_q = sub_len // Q_BLOCK_A
    scale = HEAD_DIM_A ** -0.5
    c = scale * math.log2(math.e)

    zeros = jnp.zeros((half, HEAD_DIM_A), BF16)
    for rr in range(n_res):
        kpad_ref[rr, 0:half, :] = zeros
        kpad_ref[rr, half + sub_len:2 * half + sub_len, :] = zeros
        kpad_ref[rr, half:half + sub_len, :] = k_ref[rr]
        vpad_ref[rr, 0:half, :] = zeros
        vpad_ref[rr, half + sub_len:2 * half + sub_len, :] = zeros
        vpad_ref[rr, half:half + sub_len, :] = v_ref[rr]

    def one_block(rr, q0, edge):
        q = q_ref[rr, pl.ds(q0, Q_BLOCK_A), :]
        kw = kpad_ref[rr, pl.ds(q0, 2 * Q_BLOCK_A), :]
        vw = vpad_ref[rr, pl.ds(q0, 2 * Q_BLOCK_A), :]
        t = lax.dot_general(q, kw, (((1,), (1,)), ((), ())), preferred_element_type=F32) + bias_ref[edge]
        m = jnp.max(t, axis=-1, keepdims=True)
        p = jnp.exp2((t - m) * c)
        l = jnp.sum(p, axis=-1, keepdims=True)
        o = jnp.dot(p.astype(BF16), vw, preferred_element_type=F32) / l
        lse = m * scale + jnp.log(l)
        if dilation == 1:
            rows = pl.ds(q0, Q_BLOCK_A)
        else:
            rows = pl.ds(q0 * dilation + res_blk * n_res + rr, Q_BLOCK_A, stride=dilation)
        o_ref[rows, :] = o
        lse_ref[rows, :] = jnp.broadcast_to(lse, (Q_BLOCK_A, LANES))

    if n_q >= CHAINS_A:
        n_it = n_q // CHAINS_A

        def body(it, carry):
            for u in range(CHAINS_A):
                if u == 0:
                    edge = jnp.where(it == 0, EDGE_FIRST, 0)
                elif u == CHAINS_A - 1:
                    edge = jnp.where(it == n_it - 1, EDGE_LAST, 0)
                else:
                    edge = 0
                one_block(0, pl.multiple_of((it * CHAINS_A + u) * Q_BLOCK_A, Q_BLOCK_A), edge)
            return carry

        lax.fori_loop(0, n_it, body, 0)
    else:
        for rr in range(n_res):
            for i in range(n_q):
                one_block(rr, i * Q_BLOCK_A, (EDGE_FIRST if i == 0 else 0) | (EDGE_LAST if i == n_q - 1 else 0))


def _alibi_band_bias(group, dilation):
    n = N_GROUPS_A * HEADS_A
    head = jnp.arange(1, n + 1, dtype=F32)
    slopes = jnp.exp2(-8.0 * head / n).reshape(N_GROUPS_A, HEADS_A)[group]
    half = Q_BLOCK_A // 2
    r = np.arange(Q_BLOCK_A)[:, None]
    c = np.arange(2 * Q_BLOCK_A)[None, :]
    rel = np.abs(c - r - half)
    valid = []
    for edge in range(4):
        v = rel <= half
        if edge & EDGE_FIRST:
            v = v & (c >= half)
        if edge & EDGE_LAST:
            v = v & (c < Q_BLOCK_A + half)
        valid.append(v)
    valid = jnp.asarray(np.stack(valid))
    bias = -slopes[:, None, None, None] * jnp.asarray(rel * dilation, F32)[None, None]
    return jnp.where(valid[None], bias, NEG_INF) * (HEAD_DIM_A ** 0.5)


def _window_attn(qkv_g, group, seq_len):
    _, _, n_seq, d, sub_len, _ = qkv_g.shape
    m = n_seq * seq_len
    n_q = sub_len // Q_BLOCK_A
    n_res = 1 if n_q >= CHAINS_A else min(d, CHAINS_A // n_q)

    def in_spec(t):
        return pl.BlockSpec((None, None, None, n_res, sub_len, HEAD_DIM_A), lambda b, h, r: (t, h, b, r, 0, 0))

    out_spec = pl.BlockSpec((None, seq_len, HEAD_DIM_A), lambda b, h, r: (h, b, 0))
    out_shape = jax.ShapeDtypeStruct((HEADS_A, m, HEAD_DIM_A), F32)
    return pl.pallas_call(
        functools.partial(_window_attn_kernel, sub_len=sub_len, dilation=d, n_res=n_res),
        grid=(n_seq, HEADS_A, d // n_res),
        in_specs=[in_spec(0), in_spec(1), in_spec(2),
                  pl.BlockSpec((None, 4, Q_BLOCK_A, 2 * Q_BLOCK_A), lambda b, h, r: (h, 0, 0, 0))],
        out_specs=[out_spec, out_spec],
        out_shape=[out_shape, out_shape],
        scratch_shapes=[
            pltpu.VMEM((n_res, sub_len + Q_BLOCK_A, HEAD_DIM_A), BF16),
            pltpu.VMEM((n_res, sub_len + Q_BLOCK_A, HEAD_DIM_A), BF16),
        ],
        compiler_params=_params("parallel", "parallel", "arbitrary"),
        name=f"window_attn_g{group}",
    )(qkv_g, qkv_g, qkv_g, _alibi_band_bias(group, d))


def _mix_out_proj_kernel(x_ref, o0_ref, o1_ref, o2_ref, l0_ref, l1_ref, l2_ref, w_ref, y_ref):
    cols = []
    for h in range(HEADS_A):
        l0, l1, l2 = l0_ref[h], l1_ref[h], l2_ref[h]
        mx = jnp.maximum(jnp.maximum(l0, l1), l2)
        e0, e1, e2 = jnp.exp(l0 - mx), jnp.exp(l1 - mx), jnp.exp(l2 - mx)
        mixed = (e0 * o0_ref[h] + e1 * o1_ref[h] + e2 * o2_ref[h]) / (e0 + e1 + e2)
        cols.append(mixed.astype(BF16))
    mixed = jnp.concatenate(cols, axis=-1)
    y_ref[...] = x_ref[...] + jnp.dot(mixed, w_ref[...], preferred_element_type=F32)


def _mix_out_proj(x, outs, lses, w_o, layer, *, tm=256):
    m, d = x.shape
    row = lambda i: (i, 0)
    return pl.pallas_call(
        _mix_out_proj_kernel,
        grid=(m // tm,),
        in_specs=[pl.BlockSpec((tm, d), row)]
        + [pl.BlockSpec((HEADS_A, tm, HEAD_DIM_A), lambda i: (0, i, 0))] * 6
        + [pl.BlockSpec((None, GROUP_WIDTH_A, d), lambda i: (layer, 0, 0))],
        out_specs=pl.BlockSpec((tm, d), row),
        out_shape=jax.ShapeDtypeStruct((m, d), F32),
        compiler_params=_params("parallel"),
        name="mix_out_proj",
    )(x, *outs, *lses, w_o)


def _latent_q_kernel(c_ref, g_ref, w_ref, cos_ref, sin_ref, q_ref):
    h = _rmsnorm(c_ref[...], g_ref[...]).astype(BF16)
    full = jnp.dot(h, w_ref[...], preferred_element_type=F32)
    half = QK_ROPE_DIM // 2
    lane = lax.broadcasted_iota(jnp.int32, cos_ref.shape, 1)
    cos_p = cos_ref[...] + pltpu.roll(cos_ref[...], QK_ROPE_DIM, 1)
    sin_p = sin_ref[...] + pltpu.roll(sin_ref[...], QK_ROPE_DIM, 1)
    first_half = lane % QK_ROPE_DIM < half
    rope_base = N_HEADS_B * QK_NOPE_DIM
    for pair in range(N_HEADS_B // 2):
        rope = full[:, rope_base + pair * LANES:rope_base + (pair + 1) * LANES]
        swapped = jnp.where(first_half, pltpu.roll(rope, LANES - half, 1), pltpu.roll(rope, half, 1))
        rot = rope * cos_p + swapped * sin_p
        for k, rot_k in enumerate((rot, pltpu.roll(rot, QK_ROPE_DIM, 1))):
            hd = 2 * pair + k
            q_ref[hd, :, 0:QK_NOPE_DIM] = full[:, hd * QK_NOPE_DIM:(hd + 1) * QK_NOPE_DIM].astype(BF16)
            q_ref[hd, :, QK_NOPE_DIM:QK_PAD_B] = jnp.where(lane < QK_ROPE_DIM, rot_k, 0.0).astype(BF16)


def _latent_kv_kernel(c_ref, kr_ref, krs_ref, g_ref, w_ref, cos_ref, sin_ref, k_ref, v_ref):
    h = _rmsnorm(c_ref[...], g_ref[...]).astype(BF16)
    kv = jnp.dot(h, w_ref[...], preferred_element_type=F32)
    k_rope = (kr_ref[...] * cos_ref[...] + krs_ref[...] * sin_ref[...]).astype(BF16)
    per = QK_NOPE_DIM + V_HEAD_DIM
    for hd in range(N_HEADS_B):
        k_ref[hd, :, 0:QK_NOPE_DIM] = kv[:, hd * per:hd * per + QK_NOPE_DIM].astype(BF16)
        k_ref[hd, :, QK_NOPE_DIM:QK_PAD_B] = k_rope
        v_ref[hd] = kv[:, hd * per + QK_NOPE_DIM:(hd + 1) * per].astype(BF16)


def _latent_attn_kernel(q_ref, k_ref, v_ref, o_ref, *, n_parts):
    c = (QK_NOPE_DIM + QK_ROPE_DIM) ** -0.5 * math.log2(math.e)
    rows = q_ref.shape[0] // n_parts
    for part in range(n_parts):
        sl = slice(part * rows, (part + 1) * rows)
        s = lax.dot_general(q_ref[sl, :], k_ref[...], (((1,), (1,)), ((), ())), preferred_element_type=F32)
        m = jnp.max(s, axis=-1, keepdims=True)
        p = jnp.exp2((s - m) * c)
        l = jnp.sum(p, axis=-1, keepdims=True)
        o = jnp.dot(p.astype(BF16), v_ref[...], preferred_element_type=F32)
        o_ref[sl, :] = (o / l).astype(o_ref.dtype)


def _rope_tables(seq_len):
    pos = jnp.arange(seq_len, dtype=F32)
    inv_freq = ROPE_THETA ** (-jnp.arange(0, QK_ROPE_DIM, 2, dtype=F32) / QK_ROPE_DIM)
    ang = pos[:, None] * inv_freq[None, :]
    cos, sin = jnp.cos(ang), jnp.sin(ang)
    zeros = jnp.zeros((seq_len, LANES - QK_ROPE_DIM), F32)
    return jnp.concatenate([cos, cos, zeros], axis=-1), jnp.concatenate([-sin, sin, zeros], axis=-1)


def _latent_weights(w_in, w_uq):
    half = QK_ROPE_DIM // 2
    d = w_in.shape[0]
    c_q = w_in[:, :Q_LORA_RANK]
    c_kv = w_in[:, Q_LORA_RANK:Q_LORA_RANK + KV_LORA_RANK]
    x1 = w_in[:, Q_LORA_RANK + KV_LORA_RANK:Q_LORA_RANK + KV_LORA_RANK + half]
    x2 = w_in[:, Q_LORA_RANK + KV_LORA_RANK + half:]
    zpad = jnp.zeros((d, LANES - QK_ROPE_DIM), w_in.dtype)
    w_in_p = jnp.concatenate([c_kv, x1, x2, zpad, x2, x1, zpad, c_q], axis=-1)

    w = w_uq.reshape(Q_LORA_RANK, N_HEADS_B, QK_NOPE_DIM + QK_ROPE_DIM)
    nope = w[:, :, :QK_NOPE_DIM].reshape(Q_LORA_RANK, -1)
    rope = w[:, :, QK_NOPE_DIM:].reshape(Q_LORA_RANK, -1)
    w_uq_p = jnp.concatenate([nope, rope], axis=-1)
    return w_in_p, w_uq_p


def _latent_qkv(c, q_norm, kv_norm, w_uq_p, w_ukv, cos_t, sin_t, layer, seq_len, *, tm=256):
    m = c.shape[0]
    n_pos_blocks = seq_len // tm
    row = lambda i: (i, 0)
    pos = lambda i: (i % n_pos_blocks, 0)
    const = lambda i: (0, 0)
    head_major = lambda i: (0, i, 0)
    q = pl.pallas_call(
        _latent_q_kernel,
        grid=(m // tm,),
        in_specs=[
            pl.BlockSpec((tm, Q_LORA_RANK), lambda i: (i, 1)),
            pl.BlockSpec((None, 1, Q_LORA_RANK), lambda i: (layer, 0, 0)),
            pl.BlockSpec(w_uq_p.shape, const),
            pl.BlockSpec((tm, LANES), pos),
            pl.BlockSpec((tm, LANES), pos),
        ],
        out_specs=pl.BlockSpec((N_HEADS_B, tm, QK_PAD_B), head_major),
        out_shape=jax.ShapeDtypeStruct((N_HEADS_B, m, QK_PAD_B), BF16),
        compiler_params=_params("parallel"),
        name="latent_q",
    )(c, q_norm.reshape(-1, 1, Q_LORA_RANK), w_uq_p, cos_t, sin_t)
    kr_block = KV_LORA_RANK // LANES
    k, v = pl.pallas_call(
        _latent_kv_kernel,
        grid=(m // tm,),
        in_specs=[
            pl.BlockSpec((tm, KV_LORA_RANK), row),
            pl.BlockSpec((tm, LANES), lambda i: (i, kr_block)),
            pl.BlockSpec((tm, LANES), lambda i: (i, kr_block + 1)),
            pl.BlockSpec((None, 1, KV_LORA_RANK), lambda i: (layer, 0, 0)),
            pl.BlockSpec((None,) + w_ukv.shape[1:], lambda i: (layer, 0, 0)),
            pl.BlockSpec((tm, LANES), pos),
            pl.BlockSpec((tm, LANES), pos),
        ],
        out_specs=[
            pl.BlockSpec((N_HEADS_B, tm, QK_PAD_B), head_major),
            pl.BlockSpec((N_HEADS_B, tm, V_HEAD_DIM), head_major),
        ],
        out_shape=[
            jax.ShapeDtypeStruct((N_HEADS_B, m, QK_PAD_B), BF16),
            jax.ShapeDtypeStruct((N_HEADS_B, m, V_HEAD_DIM), BF16),
        ],
        compiler_params=_params("parallel"),
        name="latent_kv",
    )(c, c, c, kv_norm.reshape(-1, 1, KV_LORA_RANK), w_ukv, cos_t, sin_t)
    return q, k, v


def _latent_attn(q, k, v, seq_len, *, part_rows=256):
    m = q.shape[1]
    n_seq = m // seq_len
    tq = min(seq_len, 2048)
    n_q = seq_len // tq
    return pl.pallas_call(
        functools.partial(_latent_attn_kernel, n_parts=tq // part_rows),
        grid=(n_seq, N_HEADS_B, n_q),
        in_specs=[
            pl.BlockSpec((None, tq, QK_PAD_B), lambda b, h, i: (h, b * n_q + i, 0)),
            pl.BlockSpec((None, seq_len, QK_PAD_B), lambda b, h, i: (h, b, 0)),
            pl.BlockSpec((None, seq_len, V_HEAD_DIM), lambda b, h, i: (h, b, 0)),
        ],
        out_specs=pl.BlockSpec((None, tq, V_HEAD_DIM), lambda b, h, i: (h, b * n_q + i, 0)),
        out_shape=jax.ShapeDtypeStruct((N_HEADS_B, m, V_HEAD_DIM), BF16),
        compiler_params=_params("parallel", "parallel", "arbitrary"),
        name="latent_attn",
    )(q, k, v)


def _out_proj_kernel(x_ref, a_ref, w_ref, y_ref):
    a = jnp.concatenate([a_ref[h] for h in range(a_ref.shape[0])], axis=-1)
    y_ref[...] = x_ref[...] + jnp.dot(a, w_ref[...], preferred_element_type=F32)


def _out_proj(x, a, w, layer, *, tm=256):
    m, d = x.shape
    n_h, _, hd = a.shape
    row = lambda i: (i, 0)
    return pl.pallas_call(
        _out_proj_kernel,
        grid=(m // tm,),
        in_specs=[
            pl.BlockSpec((tm, d), row),
            pl.BlockSpec((n_h, tm, hd), lambda i: (0, i, 0)),
            pl.BlockSpec((None, n_h * hd, d), lambda i: (layer, 0, 0)),
        ],
        out_specs=pl.BlockSpec((tm, d), row),
        out_shape=jax.ShapeDtypeStruct((m, d), F32),
        compiler_params=_params("parallel"),
        name="out_proj",
    )(x, a, w)


def _dilated_mixture_mixer(x, p, i, j, seq_len):
    qkv = _qkv_proj(x, p['mix_norm'], p['a_w_qkv'], i, j, seq_len)
    outs, lses = [], []
    for group in range(N_GROUPS_A):
        o, lse = _window_attn(qkv[group], group, seq_len)
        outs.append(o)
        lses.append(lse)
    return _mix_out_proj(x, outs, lses, p['a_w_o'], j)


def _latent_attention_mixer(x, p, i, j, seq_len):
    w_in_p, w_uq_p = p['b_latent'][j]
    c = _norm_proj(x, p['mix_norm'][i], w_in_p, F32)
    cos_t, sin_t = _rope_tables(seq_len)
    q, k, v = _latent_qkv(c, p['b_q_norm'], p['b_kv_norm'], w_uq_p, p['b_w_ukv'], cos_t, sin_t, j, seq_len)
    o = _latent_attn(q, k, v, seq_len)
    return _out_proj(x, o, p['b_w_o'], j)


def _trunk(x, seq_len, p, ffn_bf16, cast_ahead):
    depth = p['mix_norm'].shape[0]
    order = [(kind, layer) for layer in range(depth) for kind in ('ffn1', 'ffn2')]

    def ffn(x, step, final_g=None):
        kind, layer = order[step]
        g = p[kind + '_norm'][layer]
        if cast_ahead and step + 1 < len(order):
            nxt_kind, nxt_layer = order[step + 1]
            x, ffn_bf16[order[step + 1]] = _ffn(x, g, ffn_bf16[order[step]], final_g,
                                                 (*p[nxt_kind + '_f32'], nxt_layer))
            return x
        return _ffn(x, g, ffn_bf16[order[step]], final_g)

    for i in range(depth):
        x = ffn(x, 2 * i)
        j = i // 2
        if i % 2 == 0:
            x = _dilated_mixture_mixer(x, p, i, j, seq_len)
        else:
            x = _latent_attention_mixer(x, p, i, j, seq_len)
        x = ffn(x, 2 * i + 1, p['final_norm'] if i == depth - 1 else None)
    return x


def kernel(x_prompt, x_sample, ffn1_norm, ffn1_w_gate, ffn1_w_up, ffn1_w_down, mix_norm, a_w_qkv, a_w_o, b_w_in, b_q_norm, b_w_uq, b_kv_norm, b_w_ukv, b_w_o, ffn2_norm, ffn2_w_gate, ffn2_w_up, ffn2_w_down, final_norm):
    n_b = b_w_in.shape[0]
    latent = [_latent_weights(b_w_in[j], b_w_uq[j]) for j in range(n_b)]
    p = dict(
        ffn1_norm=ffn1_norm, ffn2_norm=ffn2_norm, mix_norm=mix_norm, final_norm=final_norm,
        b_q_norm=b_q_norm, b_kv_norm=b_kv_norm,
        ffn1_f32=(ffn1_w_gate, ffn1_w_up, ffn1_w_down), ffn2_f32=(ffn2_w_gate, ffn2_w_up, ffn2_w_down),
        a_w_qkv=a_w_qkv.astype(BF16), a_w_o=a_w_o.astype(BF16),
        b_latent=[(w_in_p.astype(BF16), w_uq_p.astype(BF16)) for w_in_p, w_uq_p in latent],
        b_w_ukv=b_w_ukv.astype(BF16), b_w_o=b_w_o.astype(BF16),
    )
    ffn_bf16 = {('ffn1', 0): tuple(w[0].astype(BF16) for w in p['ffn1_f32'])}
    outs = []
    for n, x in enumerate((x_prompt, x_sample)):
        b, s, d = x.shape
        outs.append(_trunk(x.reshape(b * s, d), s, p, ffn_bf16, cast_ahead=(n == 0)).reshape(b, s, d))
    return tuple(outs)
```
